```python
import math
import jax
import jax.numpy as jnp
from jax import lax
import numpy as np

D_MODEL = 1024
BATCH = 2
SEQ = 8192
DEPTH = 4
DEC_BATCH = 128
DEC_SEQ = 8
PAST_LEN = 2048
PAGE_SIZE = 128

N_MIXERS = 3
N_A = (DEPTH + 2) // N_MIXERS
N_B = (DEPTH + 1) // N_MIXERS
N_C = DEPTH // N_MIXERS

H_A = 8
DH_A = D_MODEL // (2 * H_A)
DV_A = 2 * DH_A
ROPE_THETA = 10000.0
Q_BLOCK = 128
ATTN_SCALE = DH_A ** -0.5
NEG_INF = -1e30

HEAD_B = 64
H_B = D_MODEL // HEAD_B
LORA_W = 64
LORA_A = 64
LORA_G = 160
GN_EPS_B = 64e-5

CONV_W = 31
LN_EPS = 1e-5

D_FF = 2816
FFN_CONV_W = 3
RMS_EPS = 1e-6

kernel_name = 'hybrid_diffattn_rwkv7_conformer_convffn_step'


def _rmsnorm(x, g):
    xf = x.astype(jnp.float32)
    y = xf * lax.rsqrt(jnp.mean(xf * xf, axis=-1, keepdims=True) + RMS_EPS)
    return (y * g.astype(jnp.float32)).astype(x.dtype)


def _layernorm(x, g, b):
    xf = x.astype(jnp.float32)
    mu = jnp.mean(xf, axis=-1, keepdims=True)
    var = jnp.mean(jnp.square(xf - mu), axis=-1, keepdims=True)
    y = (xf - mu) * lax.rsqrt(var + LN_EPS)
    return (y * g.astype(jnp.float32) + b.astype(jnp.float32)).astype(x.dtype)


def _modulate(x, g, shift, scale):
    return _rmsnorm(x, g) * (1.0 + scale[:, None]) + shift[:, None]


def _rope(x, pos):
    d = x.shape[-1]
    half = d // 2
    inv = jnp.exp(-math.log(ROPE_THETA) * jnp.arange(half, dtype=jnp.float32) * (2.0 / d))
    ang = pos.astype(jnp.float32)[:, None] * inv[None, :]
    cos = jnp.cos(ang)[:, None, :]
    sin = jnp.sin(ang)[:, None, :]
    xf = x.astype(jnp.float32)
    x1, x2 = xf[..., :half], xf[..., half:]
    return jnp.concatenate([x1 * cos - x2 * sin, x2 * cos + x1 * sin], axis=-1).astype(x.dtype)


def _dwconv(xpad, w):
    return lax.conv_general_dilated(xpad, w[:, None, :].astype(xpad.dtype), (1,), 'VALID',
                                    dimension_numbers=('NWC', 'WIO', 'NWC'),
                                    feature_group_count=xpad.shape[-1])


def _diff_mix(s, v, lam, subln_g, lam_init):
    B, _, Q, K = s.shape
    p = jax.nn.softmax(s, axis=-1).reshape(B, H_A, 2, Q, K)
    a = (p[:, :, 0] - lam * p[:, :, 1]).astype(v.dtype)
    o = jnp.einsum('bhqk,bkhd->bqhd', a, v)
    return _rmsnorm(o, subln_g) * (1.0 - lam_init)


def _diff_attn_prompt(q, k, v, lam, subln_g, lam_init):
    B, S = q.shape[:2]
    nb = S // Q_BLOCK
    qb = jnp.moveaxis(q.reshape(B, nb, Q_BLOCK, 2 * H_A, DH_A), 1, 0)
    kpos = jnp.arange(S)

    def one_block(args):
        qi, bi = args
        s = jnp.einsum('bqhd,bkhd->bhqk', qi, k, preferred_element_type=jnp.float32) * ATTN_SCALE
        qpos = bi * Q_BLOCK + jnp.arange(Q_BLOCK)
        s = jnp.where(kpos[None, :] <= qpos[:, None], s, NEG_INF)
        return _diff_mix(s, v, lam, subln_g, lam_init)

    o = lax.map(one_block, (qb, jnp.arange(nb)))
    return jnp.moveaxis(o, 0, 1).reshape(B, S, H_A * DV_A)


def _diff_attn_sample(q, k_new, v_new, k_pool, v_pool, page_table, lam, subln_g, lam_init):
    DB, T = q.shape[:2]
    past = page_table.shape[1] * PAGE_SIZE
    k_past = k_pool[page_table].reshape(DB, past, 2 * H_A, DH_A)
    v_past = v_pool[page_table].reshape(DB, past, H_A, DV_A)
    k_all = jnp.concatenate([k_past.astype(k_new.dtype), k_new], axis=1)
    v_all = jnp.concatenate([v_past.astype(v_new.dtype), v_new], axis=1)
    s = jnp.einsum('bqhd,bkhd->bhqk', q, k_all, preferred_element_type=jnp.float32) * ATTN_SCALE
    kpos = jnp.arange(past + T)
    qpos = past + jnp.arange(T)
    s = jnp.where(kpos[None, :] <= qpos[:, None], s, NEG_INF)
    o = _diff_mix(s, v_all, lam, subln_g, lam_init)
    return o.reshape(DB, T, H_A * DV_A)


def _diff_attention(h, pos, j, layer_idx, P, paged):
    B, T, _ = h.shape
    f32 = jnp.float32
    qkv = h @ P['a_w_qkv'][j]
    nqk = 2 * H_A * DH_A
    q = _rope(qkv[..., :nqk].reshape(B, T, 2 * H_A, DH_A), pos)
    k = _rope(qkv[..., nqk:2 * nqk].reshape(B, T, 2 * H_A, DH_A), pos)
    v = qkv[..., 2 * nqk:].reshape(B, T, H_A, DV_A)
    lam_init = 0.8 - 0.6 * math.exp(-0.3 * layer_idx)
    lam = (jnp.exp(jnp.sum(P['a_lq1'][j].astype(f32) * P['a_lk1'][j].astype(f32)))
           - jnp.exp(jnp.sum(P['a_lq2'][j].astype(f32) * P['a_lk2'][j].astype(f32))) + lam_init)
    if paged is None:
        o = _diff_attn_prompt(q, k, v, lam, P['a_subln'][j], lam_init)
    else:
        cache_k, cache_v, page_table = paged
        o = _diff_attn_sample(q, k, v, cache_k[j], cache_v[j], page_table, lam, P['a_subln'][j], lam_init)
    return o @ P['a_w_o'][j], k, v


def _rwkv7_step(S, inp):
    r, w, k, v, kk, a = inp
    sa = jnp.einsum('bhvk,bhk->bhv', S, -kk)
    S = S * w[:, :, None, :] + sa[..., None] * (kk * a)[:, :, None, :] + v[..., None] * k[:, :, None, :]
    return S, jnp.einsum('bhvk,bhk->bhv', S, r)


def _rwkv7_tmix(h, shift_prev, wkv_prev, P, j):
    B, T, D = h.shape
    f32 = jnp.float32
    mu = P['b_mu'][j]
    xx = jnp.concatenate([shift_prev[:, None, :].astype(h.dtype), h[:, :-1]], axis=1) - h
    xr, xw, xk, xv, xa, xg = [h + xx * mu[n] for n in range(6)]
    r = xr @ P['b_w_r'][j]
    k = xk @ P['b_w_k'][j]
    v = xv @ P['b_w_v'][j]
    w = -jax.nn.softplus(-(P['b_w0'][j] + jnp.tanh(xw @ P['b_w1'][j]) @ P['b_w2'][j])) - 0.5
    a = jax.nn.sigmoid(P['b_a0'][j] + (xa @ P['b_a1'][j]) @ P['b_a2'][j])
    g = jax.nn.sigmoid(xg @ P['b_g1'][j]) @ P['b_g2'][j]
    heads = lambda t: t.reshape(B, T, H_B, HEAD_B).astype(f32)
    kk = heads(k * P['b_k_k'][j])
    kk = kk * lax.rsqrt(jnp.maximum(jnp.sum(kk * kk, axis=-1, keepdims=True), 1e-24))
    k = k * (1.0 + (a - 1.0) * P['b_k_a'][j])
    rh, kh, vh, ah = heads(r), heads(k), heads(v), heads(a)
    decay = jnp.exp(-jnp.exp(heads(w)))
    seq = tuple(jnp.moveaxis(t, 1, 0) for t in (rh, decay, kh, vh, kk, ah))
    wkv_last, y = lax.scan(_rwkv7_step, wkv_prev.astype(f32), seq)
    y = jnp.moveaxis(y, 0, 1)
    mean = jnp.mean(y, axis=-1, keepdims=True)
    var = jnp.mean(jnp.square(y - mean), axis=-1, keepdims=True)
    yn = ((y - mean) * lax.rsqrt(var + GN_EPS_B)).reshape(B, T, D)
    yn = yn * P['b_ln_g'][j].astype(f32) + P['b_ln_b'][j].astype(f32)
    bonus = (jnp.sum(rh * kh * P['b_r_k'][j].astype(f32), axis=-1, keepdims=True) * vh).reshape(B, T, D)
    out = ((yn + bonus).astype(h.dtype) * g) @ P['b_w_o'][j]
    return out, h[:, -1], wkv_last.astype(wkv_prev.dtype)


def _conformer_conv(h, buf, P, j):
    D = h.shape[-1]
    u = h @ P['cv_w_pw1'][j] + P['cv_b_pw1'][j]
    u = u[..., :D] * jax.nn.sigmoid(u[..., D:])
    up = jnp.concatenate([buf.astype(u.dtype), u], axis=1)
    y = _dwconv(up, P['cv_w_dw'][j]) + P['cv_b_dw'][j]
    y = jax.nn.silu(_layernorm(y, P['cv_ln_g'][j], P['cv_ln_b'][j]))
    out = y @ P['cv_w_pw2'][j] + P['cv_b_pw2'][j]
    return out, up[:, -(CONV_W - 1):]


def _conv_ffn(h, buf, P, i):
    u = h @ P['f_w_in'][i]
    act, val = u[..., :D_FF], u[..., D_FF:]
    ap = jnp.concatenate([buf.astype(act.dtype), act], axis=1)
    act = _dwconv(ap, P['f_w_dw'][i]) + P['f_b_dw'][i]
    return (jax.nn.silu(act) * val) @ P['f_w_out'][i], ap[:, -(FFN_CONV_W - 1):]


def _trunk(x, c, pos, shift0, wkv0, conv0, ffn0, P, paged):
    new_k, new_v, new_wkv, new_shift, new_conv, new_ffn = [], [], [], [], [], []
    for i in range(DEPTH):
        mod = jax.nn.silu(c) @ P['w_ada'][i] + P['b_ada'][i]
        sh1, sc1, g1, sh2, sc2, g2 = jnp.split(mod, 6, axis=-1)
        h = _modulate(x, P['norm_mix'][i], sh1, sc1)
        kind, j = i % N_MIXERS, i // N_MIXERS
        if kind == 0:
            o, k, v = _diff_attention(h, pos, j, i, P, paged)
            new_k.append(k)
            new_v.append(v)
        elif kind == 1:
            o, sh, wkv = _rwkv7_tmix(h, shift0[j], wkv0[j], P, j)
            new_shift.append(sh)
            new_wkv.append(wkv)
        else:
            o, cb = _conformer_conv(h, conv0[j], P, j)
            new_conv.append(cb)
        x = x + g1[:, None] * o
        h = _modulate(x, P['norm_ffn'][i], sh2, sc2)
        o, fb = _conv_ffn(h, ffn0[i], P, i)
        new_ffn.append(fb)
        x = x + g2[:, None] * o
    y = _rmsnorm(x, P['norm_out'])
    return (y, jnp.stack(new_k), jnp.stack(new_v), jnp.stack(new_wkv), jnp.stack(new_shift),
            jnp.stack(new_conv), jnp.stack(new_ffn))


def setup_inputs(seed: int = 0) -> dict:
    key = jax.random.key(seed)
    ks = iter(jax.random.split(key, 64))
    f32 = jnp.float32
    D = D_MODEL

    def nrm(shape, scale):
        return jax.random.normal(next(ks), shape, f32) * scale

    def gain(shape):
        return 1.0 + nrm(shape, 0.1)

    n_pages = PAST_LEN // PAGE_SIZE
    n_used = DEC_BATCH * n_pages
    n_pool = n_used + n_used // 4
    page_table = jax.random.permutation(next(ks), n_pool)[:n_used].reshape(DEC_BATCH, n_pages).astype(jnp.int32)
    return {
        'x_prompt': nrm((BATCH, SEQ, D), 1.0),
        'x_sample': nrm((DEC_BATCH, DEC_SEQ, D), 1.0),
        'cache_k': nrm((N_A, n_pool, PAGE_SIZE, 2 * H_A, DH_A), 1.0),
        'cache_v': nrm((N_A, n_pool, PAGE_SIZE, H_A, DV_A), 1.0),
        'state_wkv': nrm((N_B, DEC_BATCH, H_B, HEAD_B, HEAD_B), 0.3),
        'state_shift': nrm((N_B, DEC_BATCH, D), 1.0),
        'state_conv': nrm((N_C, DEC_BATCH, CONV_W - 1, D), 0.5),
        'state_ffn': nrm((DEPTH, DEC_BATCH, FFN_CONV_W - 1, D_FF), 0.5),
        'page_table': page_table,
        'c_prompt': nrm((BATCH, D), 1.0),
        'c_sample': nrm((DEC_BATCH, D), 1.0),
        'norm_mix': gain((DEPTH, D)),
        'norm_ffn': gain((DEPTH, D)),
        'w_ada': nrm((DEPTH, D, 6 * D), 0.5 * D ** -0.5),
        'b_ada': nrm((DEPTH, 6 * D), 0.2),
        'a_w_qkv': nrm((N_A, D, 4 * H_A * DH_A + H_A * DV_A), D ** -0.5),
        'a_w_o': nrm((N_A, H_A * DV_A, D), (H_A * DV_A) ** -0.5),
        'a_lq1': nrm((N_A, DH_A), 0.1),
        'a_lk1': nrm((N_A, DH_A), 0.1),
        'a_lq2': nrm((N_A, DH_A), 0.1),
        'a_lk2': nrm((N_A, DH_A), 0.1),
        'a_subln': gain((N_A, DV_A)),
        'b_mu': jax.random.uniform(next(ks), (N_B, 6, D), f32),
        'b_w_r': nrm((N_B, D, D), D ** -0.5),
        'b_w_k': nrm((N_B, D, D), D ** -0.5),
        'b_w_v': nrm((N_B, D, D), D ** -0.5),
        'b_w_o': nrm((N_B, D, D), D ** -0.5),
        'b_w0': -1.0 + nrm((N_B, D), 0.5),
        'b_w1': nrm((N_B, D, LORA_W), D ** -0.5),
        'b_w2': nrm((N_B, LORA_W, D), 0.5 * LORA_W ** -0.5),
        'b_a0': nrm((N_B, D), 0.5),
        'b_a1': nrm((N_B, D, LORA_A), D ** -0.5),
        'b_a2': nrm((N_B, LORA_A, D), 0.5 * LORA_A ** -0.5),
        'b_g1': nrm((N_B, D, LORA_G), D ** -0.5),
        'b_g2': nrm((N_B, LORA_G, D), LORA_G ** -0.5),
        'b_k_k': 0.85 + nrm((N_B, D), 0.1),
        'b_k_a': gain((N_B, D)),
        'b_r_k': nrm((N_B, H_B, HEAD_B), 0.1),
        'b_ln_g': gain((N_B, D)),
        'b_ln_b': nrm((N_B, D), 0.02),
        'cv_w_pw1': nrm((N_C, D, 2 * D), D ** -0.5),
        'cv_b_pw1': nrm((N_C, 2 * D), 0.02),
        'cv_w_dw': nrm((N_C, CONV_W, D), CONV_W ** -0.5),
        'cv_b_dw': nrm((N_C, D), 0.02),
        'cv_ln_g': gain((N_C, D)),
        'cv_ln_b': nrm((N_C, D), 0.02),
        'cv_w_pw2': nrm((N_C, D, D), D ** -0.5),
        'cv_b_pw2': nrm((N_C, D), 0.02),
        'f_w_in': nrm((DEPTH, D, 2 * D_FF), D ** -0.5),
        'f_w_dw': nrm((DEPTH, FFN_CONV_W, D_FF), FFN_CONV_W ** -0.5),
        'f_b_dw': nrm((DEPTH, D_FF), 0.02),
        'f_w_out': nrm((DEPTH, D_FF, D), D_FF ** -0.5),
        'norm_out': gain((D,)),
    }


def reference(x_prompt, x_sample, cache_k, cache_v, state_wkv, state_shift, state_conv, state_ffn,
              page_table, c_prompt, c_sample, norm_mix, norm_ffn, w_ada, b_ada,
              a_w_qkv, a_w_o, a_lq1, a_lk1, a_lq2, a_lk2, a_subln,
              b_mu, b_w_r, b_w_k, b_w_v, b_w_o, b_w0, b_w1, b_w2, b_a0, b_a1, b_a2, b_g1, b_g2,
              b_k_k, b_k_a, b_r_k, b_ln_g, b_ln_b,
              cv_w_pw1, cv_b_pw1, cv_w_dw, cv_b_dw, cv_ln_g, cv_ln_b, cv_w_pw2, cv_b_pw2,
              f_w_in, f_w_dw, f_b_dw, f_w_out, norm_out):
    P = dict(norm_mix=norm_mix, norm_ffn=norm_ffn, w_ada=w_ada, b_ada=b_ada,
             a_w_qkv=a_w_qkv, a_w_o=a_w_o, a_lq1=a_lq1, a_lk1=a_lk1, a_lq2=a_lq2, a_lk2=a_lk2,
             a_subln=a_subln, b_mu=b_mu, b_w_r=b_w_r, b_w_k=b_w_k, b_w_v=b_w_v, b_w_o=b_w_o,
             b_w0=b_w0, b_w1=b_w1, b_w2=b_w2, b_a0=b_a0, b_a1=b_a1, b_a2=b_a2, b_g1=b_g1, b_g2=b_g2,
             b_k_k=b_k_k, b_k_a=b_k_a, b_r_k=b_r_k, b_ln_g=b_ln_g, b_ln_b=b_ln_b,
             cv_w_pw1=cv_w_pw1, cv_b_pw1=cv_b_pw1, cv_w_dw=cv_w_dw, cv_b_dw=cv_b_dw,
             cv_ln_g=cv_ln_g, cv_ln_b=cv_ln_b, cv_w_pw2=cv_w_pw2, cv_b_pw2=cv_b_pw2,
             f_w_in=f_w_in, f_w_dw=f_w_dw, f_b_dw=f_b_dw, f_w_out=f_w_out, norm_out=norm_out)
    B, S, D = x_prompt.shape
    dt = x_prompt.dtype
    y_prompt, k_prompt, v_prompt, wkv_prompt, shift_prompt, conv_prompt, ffn_prompt = _trunk(
        x_prompt, c_prompt, jnp.arange(S),
        jnp.zeros((N_B, B, D), dt), jnp.zeros((N_B, B, H_B, HEAD_B, HEAD_B), dt),
        jnp.zeros((N_C, B, CONV_W - 1, D), dt), jnp.zeros((DEPTH, B, FFN_CONV_W - 1, D_FF), dt),
        P, None)
    past = page_table.shape[1] * PAGE_SIZE
    y_sample, k_sample, v_sample, wkv_sample, shift_sample, conv_sample, ffn_sample = _trunk(
        x_sample, c_sample, past + jnp.arange(x_sample.shape[1]),
        state_shift, state_wkv, state_conv, state_ffn, P, (cache_k, cache_v, page_table))
    return (y_prompt, y_sample, k_prompt, v_prompt, wkv_prompt, shift_prompt, conv_prompt, ffn_prompt,
            k_sample, v_sample, wkv_sample, shift_sample, conv_sample, ffn_sample)
```

```python
import functools
import math

import numpy as np
import jax
import jax.numpy as jnp
from jax import lax
from jax.experimental import pallas as pl
from jax.experimental.pallas import tpu as pltpu

F32 = jnp.float32
BF16 = jnp.bfloat16

N_MIXERS = 3
H_A = 8
DH_A = 64
DV_A = 128
ROPE_THETA = 10000.0
PAGE_SIZE = 128
NEG_INF = -1e30
HEAD_B = 64
GN_EPS_B = 64e-5
LN_EPS = 1e-5
RMS_EPS = 1e-6
CONV_W = 31
FFN_CONV_W = 3

SUBLANES = 8
LANES = 128
VMEM_LIMIT_BYTES = 56 * 1024 * 1024


def _cparams(*sem):
    return pltpu.CompilerParams(dimension_semantics=sem, vmem_limit_bytes=VMEM_LIMIT_BYTES)


def _resident(shape):
    nd = len(shape)
    return pl.BlockSpec(shape, lambda *_: (0,) * nd, pipeline_mode=pl.Buffered(1))


def _dot(a, b):
    return jnp.dot(a, b, preferred_element_type=F32)


def _dot_nt(a, b):
    return lax.dot_general(a, b, (((1,), (1,)), ((), ())), preferred_element_type=F32)


def _dot_tn(a, b):
    return lax.dot_general(a, b, (((0,), (0,)), ((), ())), preferred_element_type=F32)


def _sigmoid(x):
    return 1.0 / (1.0 + jnp.exp(-x))


def _silu(x):
    return x * _sigmoid(x)


def _bc_rows(m, tm):
    nb, c = m.shape
    if nb == 1 or nb == tm:
        return m
    return jnp.broadcast_to(m[None], (tm // nb, nb, c)).reshape(tm, c)


def _modnorm(x, gain, shift, scale):
    tm = x.shape[0]
    y = x * lax.rsqrt(jnp.mean(x * x, axis=-1, keepdims=True) + RMS_EPS) * gain
    return y * (1.0 + _bc_rows(scale, tm)) + _bc_rows(shift, tm)


class _Group:
    def __init__(self, nb, rows, sd, mod, has_state):
        self.nb = nb
        self.rows = rows
        self.sd = sd
        self.mod = mod
        self.has_state = has_state

    def mod_spec(self, layer, k, d):
        if self.mod.ndim == 4:
            return pl.BlockSpec((None, None, 1, d), lambda b, j: (layer, b, 0, k))
        return pl.BlockSpec((None, self.mod.shape[1], d), lambda b, j: (layer, 0, k))

    def row_spec(self, tm, c):
        return pl.BlockSpec((None, tm, c), lambda b, j: (b, j, 0))


def _adaln_body(c_ref, w_ref, b_ref, o_ref):
    sc = _silu(c_ref[...]).astype(BF16)
    o_ref[...] = _dot(sc, w_ref[...].astype(BF16)) + b_ref[...]


def _adaln(c_all, w_ada, b_ada):
    depth, d, n = w_ada.shape
    bp = c_all.shape[0]
    tn = 1536
    return pl.pallas_call(
        _adaln_body,
        grid=(depth, n // tn),
        in_specs=[pl.BlockSpec((bp, d), lambda i, j: (0, 0)),
                  pl.BlockSpec((None, d, tn), lambda i, j: (i, 0, j)),
                  pl.BlockSpec((None, 1, tn), lambda i, j: (i, 0, j))],
        out_specs=pl.BlockSpec((None, bp, tn), lambda i, j: (i, 0, j)),
        out_shape=jax.ShapeDtypeStruct((depth, bp, n), F32),
        compiler_params=_cparams("arbitrary", "arbitrary"),
        name="adaln",
    )(c_all, w_ada, b_ada.reshape(depth, 1, n))


def _pre_a_body(x_ref, sh_ref, sc_ref, gain_ref, w_ref, cos_ref, sina_ref, sinb_ref,
                k_out, v_out, q_out, kh_out, vh_out, *, head_major):
    h = _modnorm(x_ref[...], gain_ref[...], sh_ref[...], sc_ref[...]).astype(BF16)
    qkv = _dot(h, w_ref[...])
    cos, sina, sinb = cos_ref[...], sina_ref[...], sinb_ref[...]
    nqk = H_A * LANES
    scale = DH_A ** -0.5
    for s in range(2 * H_A):
        xs = qkv[:, LANES * s:LANES * (s + 1)]
        ro = xs * cos + pltpu.roll(xs, LANES - 32, 1) * sina + pltpu.roll(xs, 32, 1) * sinb
        if s < H_A:
            qv = (ro * scale).astype(BF16)
            if head_major:
                q_out[s] = qv
            else:
                q_out[:, LANES * s:LANES * (s + 1)] = qv
        else:
            hh = s - H_A
            k_out[:, LANES * hh:LANES * (hh + 1)] = ro
            if head_major:
                kh_out[hh] = ro.astype(BF16)
    v = qkv[:, 2 * nqk:]
    v_out[...] = v
    if head_major:
        for hh in range(H_A):
            vh_out[hh] = v[:, LANES * hh:LANES * (hh + 1)].astype(BF16)


def _rope_tables(pos):
    half = DH_A // 2
    inv = jnp.exp(-math.log(ROPE_THETA) * jnp.arange(half, dtype=F32) * (2.0 / DH_A))
    ang = pos.astype(F32)[:, None] * inv[None, :]
    cos, sin = jnp.cos(ang), jnp.sin(ang)
    zero = jnp.zeros_like(sin)
    cos_t = jnp.concatenate([cos, cos, cos, cos], axis=1)
    sina = jnp.concatenate([-sin, zero, -sin, zero], axis=1)
    sinb = jnp.concatenate([zero, sin, zero, sin], axis=1)
    return cos_t, sina, sinb


def _pre_a(grp, layer, x, gain, w_qkv, tabs, tm, head_major):
    nb, rows, d = x.shape
    n = w_qkv.shape[1]
    dk = H_A * LANES
    kern = functools.partial(_pre_a_body, head_major=head_major)
    tab_spec = pl.BlockSpec((tm, LANES), lambda b, j: (j, 0))
    in_specs = [grp.row_spec(tm, d), grp.mod_spec(layer, 0, d), grp.mod_spec(layer, 1, d),
                _resident((1, d)), _resident((d, n)), tab_spec, tab_spec, tab_spec]
    out_specs = [grp.row_spec(tm, dk), grp.row_spec(tm, dk)]
    out_shape = [jax.ShapeDtypeStruct((nb, rows, dk), F32), jax.ShapeDtypeStruct((nb, rows, dk), F32)]
    if head_major:
        hspec = pl.BlockSpec((None, H_A, tm, LANES), lambda b, j: (b, 0, j, 0))
        out_specs += [hspec, hspec, hspec]
        out_shape += [jax.ShapeDtypeStruct((nb, H_A, rows, LANES), BF16)] * 3
    else:
        dummy = pl.BlockSpec((None, SUBLANES, LANES), lambda b, j: (b, 0, 0))
        out_specs += [grp.row_spec(tm, dk), dummy, dummy]
        out_shape += [jax.ShapeDtypeStruct((nb, rows, dk), BF16),
                      jax.ShapeDtypeStruct((nb, SUBLANES, LANES), BF16),
                      jax.ShapeDtypeStruct((nb, SUBLANES, LANES), BF16)]
    return pl.pallas_call(
        kern, grid=(nb, rows // tm), in_specs=in_specs, out_specs=out_specs, out_shape=out_shape,
        compiler_params=_cparams("arbitrary", "arbitrary"), name="pre_attn",
    )(x, grp.mod, grp.mod, gain, w_qkv, *tabs)


def _lam(lq1, lk1, lq2, lk2, lam_init):
    return (jnp.exp(jnp.sum(lq1 * lk1, axis=-1, keepdims=True))
            - jnp.exp(jnp.sum(lq2 * lk2, axis=-1, keepdims=True)) + lam_init)


def _diff_finish(acc, l, lam, g, lam_init, n):
    o = acc[0:n] / l[0:n] - lam * (acc[n:2 * n] / l[n:2 * n])
    return o * lax.rsqrt(jnp.mean(o * o, axis=-1, keepdims=True) + RMS_EPS) * g * (1.0 - lam_init)


def _online_update(s, v, m_prev, l_prev, acc_prev):
    m_new = jnp.maximum(m_prev, jnp.max(s, axis=-1, keepdims=True))
    alpha = jnp.exp(m_prev - m_new)
    p = jnp.exp(s - m_new)
    l_new = alpha * l_prev + jnp.sum(p, axis=-1, keepdims=True)
    acc_new = alpha * acc_prev + _dot(p.astype(BF16), v)
    return m_new, l_new, acc_new


def _attn_p_body(qi_tab, ki_tab, q_ref, k_ref, v_ref, lq1, lk1, lq2, lk2, g_ref, o_ref,
                 qs_sc, m_sc, l_sc, acc_sc, *, tq, lam_init):
    t = pl.program_id(2)
    qi = qi_tab[t]
    ki = ki_tab[t]

    @pl.when(ki == 0)
    def _():
        q = q_ref[...]
        lane = lax.broadcasted_iota(jnp.int32, q.shape, 1)
        zero = jnp.zeros_like(q)
        qs_sc[0:tq, :] = jnp.where(lane < DH_A, q, zero)
        qs_sc[tq:2 * tq, :] = jnp.where(lane >= DH_A, q, zero)
        m_sc[...] = jnp.full(m_sc.shape, NEG_INF, F32)
        l_sc[...] = jnp.zeros(l_sc.shape, F32)
        acc_sc[...] = jnp.zeros(acc_sc.shape, F32)

    def step(diagonal):
        s = _dot_nt(qs_sc[...], k_ref[...])
        if diagonal:
            row = lax.broadcasted_iota(jnp.int32, s.shape, 0)
            col = lax.broadcasted_iota(jnp.int32, s.shape, 1)
            qrow = jnp.where(row >= tq, row - tq, row)
            s = jnp.where(col <= qrow, s, NEG_INF)
        m_new, l_new, acc_new = _online_update(s, v_ref[...], m_sc[...], l_sc[...], acc_sc[...])
        m_sc[...] = m_new
        l_sc[...] = l_new
        acc_sc[...] = acc_new

    @pl.when(ki < qi)
    def _():
        step(False)

    @pl.when(ki == qi)
    def _():
        step(True)
        lam = _lam(lq1[...], lk1[...], lq2[...], lk2[...], lam_init)
        o_ref[...] = _diff_finish(acc_sc[...], l_sc[...], lam, g_ref[...], lam_init, tq).astype(BF16)


def _attn_prompt(qh, kh, vh, lq1, lk1, lq2, lk2, g, lam_init, tq):
    nb, nh, s, _ = qh.shape
    nq = s // tq
    pairs = [(qi, ki) for qi in range(nq) for ki in range(qi + 1)]
    qi_tab = jnp.asarray(np.array([p[0] for p in pairs], np.int32))
    ki_tab = jnp.asarray(np.array([p[1] for p in pairs], np.int32))
    small = pl.BlockSpec((1, DH_A), lambda b, h, t, qt, kt: (0, 0))
    grid_spec = pltpu.PrefetchScalarGridSpec(
        num_scalar_prefetch=2,
        grid=(nb, nh, len(pairs)),
        in_specs=[pl.BlockSpec((None, None, tq, LANES), lambda b, h, t, qt, kt: (b, h, qt[t], 0)),
                  pl.BlockSpec((None, None, tq, LANES), lambda b, h, t, qt, kt: (b, h, kt[t], 0)),
                  pl.BlockSpec((None, None, tq, LANES), lambda b, h, t, qt, kt: (b, h, kt[t], 0)),
                  small, small, small, small,
                  pl.BlockSpec((1, DV_A), lambda b, h, t, qt, kt: (0, 0))],
        out_specs=pl.BlockSpec((None, tq, LANES), lambda b, h, t, qt, kt: (b, qt[t], h)),
        scratch_shapes=[pltpu.VMEM((2 * tq, LANES), BF16), pltpu.VMEM((2 * tq, 1), F32),
                        pltpu.VMEM((2 * tq, 1), F32), pltpu.VMEM((2 * tq, DV_A), F32)])
    return pl.pallas_call(
        functools.partial(_attn_p_body, tq=tq, lam_init=lam_init),
        grid_spec=grid_spec,
        out_shape=jax.ShapeDtypeStruct((nb, s, nh * LANES), BF16),
        compiler_params=_cparams("arbitrary", "arbitrary", "arbitrary"), name="attn_prompt",
    )(qi_tab, ki_tab, qh, kh, vh, lq1, lk1, lq2, lk2, g)


def _attn_s_body(pt, q_ref, kn_ref, vn_ref, kp_ref, vp_ref, lq1, lk1, lq2, lk2, g_ref, o_ref,
                 m_sc, l_sc, acc_sc, *, lam_init):
    p = pl.program_id(1)
    n_pages = pl.num_programs(1)
    t_new = q_ref.shape[0]

    @pl.when(p == 0)
    def _():
        m_sc[...] = jnp.full(m_sc.shape, NEG_INF, F32)
        l_sc[...] = jnp.zeros(l_sc.shape, F32)
        acc_sc[...] = jnp.zeros(acc_sc.shape, F32)

    q = q_ref[...]

    def head_scores(hh, k1, k2):
        q1 = q[:, LANES * hh:LANES * hh + DH_A]
        q2 = q[:, LANES * hh + DH_A:LANES * (hh + 1)]
        return jnp.concatenate([_dot_nt(q1, k1), _dot_nt(q2, k2)], axis=0)

    for hh in range(H_A):
        k1 = kp_ref[:, 2 * hh, :].astype(BF16)
        k2 = kp_ref[:, 2 * hh + 1, :].astype(BF16)
        v = vp_ref[:, hh, :].astype(BF16)
        m_new, l_new, acc_new = _online_update(head_scores(hh, k1, k2), v,
                                               m_sc[hh], l_sc[hh], acc_sc[hh])
        m_sc[hh] = m_new
        l_sc[hh] = l_new
        acc_sc[hh] = acc_new

    @pl.when(p == n_pages - 1)
    def _():
        lam = _lam(lq1[...], lk1[...], lq2[...], lk2[...], lam_init)
        kn = kn_ref[...].astype(BF16)
        vn = vn_ref[...].astype(BF16)
        for hh in range(H_A):
            k1 = kn[:, LANES * hh:LANES * hh + DH_A]
            k2 = kn[:, LANES * hh + DH_A:LANES * (hh + 1)]
            s = head_scores(hh, k1, k2)
            row = lax.broadcasted_iota(jnp.int32, s.shape, 0)
            col = lax.broadcasted_iota(jnp.int32, s.shape, 1)
            qrow = jnp.where(row >= t_new, row - t_new, row)
            s = jnp.where(col <= qrow, s, NEG_INF)
            _, l_new, acc_new = _online_update(s, vn[:, LANES * hh:LANES * (hh + 1)],
                                               m_sc[hh], l_sc[hh], acc_sc[hh])
            o = _diff_finish(acc_new, l_new, lam, g_ref[...], lam_init, t_new)
            o_ref[:, LANES * hh:LANES * (hh + 1)] = o.astype(BF16)


def _attn_sample(q, k_new, v_new, cache_k, cache_v, page_table, j, lq1, lk1, lq2, lk2, g, lam_init):
    t_new, width = q.shape
    db, n_pages = page_table.shape
    dk = width // db
    small = pl.BlockSpec((1, DH_A), lambda b, p, pt: (0, 0))
    seq = pl.BlockSpec((t_new, dk), lambda b, p, pt: (0, b))
    grid_spec = pltpu.PrefetchScalarGridSpec(
        num_scalar_prefetch=1,
        grid=(db, n_pages),
        in_specs=[seq, seq, seq,
                  pl.BlockSpec((None, None, PAGE_SIZE, 2 * H_A, DH_A),
                               lambda b, p, pt: (j, pt[b, p], 0, 0, 0)),
                  pl.BlockSpec((None, None, PAGE_SIZE, H_A, DV_A),
                               lambda b, p, pt: (j, pt[b, p], 0, 0, 0)),
                  small, small, small, small,
                  pl.BlockSpec((1, DV_A), lambda b, p, pt: (0, 0))],
        out_specs=seq,
        scratch_shapes=[pltpu.VMEM((H_A, 2 * t_new, 1), F32), pltpu.VMEM((H_A, 2 * t_new, 1), F32),
                        pltpu.VMEM((H_A, 2 * t_new, DV_A), F32)])
    return pl.pallas_call(
        functools.partial(_attn_s_body, lam_init=lam_init),
        grid_spec=grid_spec,
        out_shape=jax.ShapeDtypeStruct((t_new, width), BF16),
        compiler_params=_cparams("arbitrary", "arbitrary"), name="attn_sample",
    )(page_table, q, k_new, v_new, cache_k, cache_v, lq1, lk1, lq2, lk2, g)


def _pre_b_body(*refs, sd, halo, has_state):
    (x_ref, sh_ref, sc_ref, gain_ref, mu_ref, wr, wk, wv, w0, w1, w2, a0, a1, a2, g1, g2) = refs[:16]
    rest = refs[16:]
    if has_state:
        st_ref, rest = rest[0], rest[1:]
    r_out, wl_out, k_out, v_out, a_out, g_out, hl_out, ext = rest
    j = pl.program_id(1)
    tm = x_ref.shape[0]

    @pl.when(j == 0)
    def _():
        if has_state:
            ext[0:halo, :] = st_ref[...]
        else:
            ext[0:halo, :] = jnp.zeros((halo, ext.shape[1]), F32)

    h = _modnorm(x_ref[...], gain_ref[...], sh_ref[...], sc_ref[...])
    ext[halo:halo + tm, :] = h
    xx = ext[pl.ds(halo - sd, tm), :] - h
    ext[0:halo, :] = ext[tm:tm + halo, :]
    hl_out[...] = h[tm - halo:tm]

    mu = mu_ref[...]
    mix = lambda n: (h + xx * mu[n:n + 1]).astype(BF16)
    r_out[...] = _dot(mix(0), wr[...])
    wraw = w0[...] + _dot(jnp.tanh(_dot(mix(1), w1[...])).astype(BF16), w2[...])
    k_out[...] = _dot(mix(2), wk[...])
    v_out[...] = _dot(mix(3), wv[...])
    a_out[...] = _sigmoid(a0[...] + _dot(_dot(mix(4), a1[...]).astype(BF16), a2[...]))
    g_out[...] = _dot(_sigmoid(_dot(mix(5), g1[...])).astype(BF16), g2[...])
    z = -wraw
    softplus = jnp.maximum(z, 0.0) + jnp.log(1.0 + jnp.exp(-jnp.abs(z)))
    wl_out[...] = -jnp.exp(-softplus - 0.5)


def _pre_b(grp, layer, x, gain, prm, state, tm):
    nb, rows, d = x.shape
    halo = max(SUBLANES, grp.sd)
    kern = functools.partial(_pre_b_body, sd=grp.sd, halo=halo, has_state=grp.has_state)
    row = lambda a: a.reshape(1, -1)
    ins = [x, grp.mod, grp.mod, gain, prm['mu'], prm['w_r'], prm['w_k'], prm['w_v'], row(prm['w0']),
           prm['w1'], prm['w2'], row(prm['a0']), prm['a1'], prm['a2'], prm['g1'], prm['g2']]
    in_specs = [grp.row_spec(tm, d), grp.mod_spec(layer, 0, d), grp.mod_spec(layer, 1, d)]
    in_specs += [_resident(a.shape) for a in ins[3:]]
    if grp.has_state:
        ins.append(state)
        in_specs.append(_resident(state.shape))
    big = jax.ShapeDtypeStruct((nb, rows, d), F32)
    out_specs = [grp.row_spec(tm, d)] * 6 + [pl.BlockSpec((None, halo, d), lambda b, j: (b, 0, 0))]
    out_shape = [big] * 6 + [jax.ShapeDtypeStruct((nb, halo, d), F32)]
    return pl.pallas_call(
        kern, grid=(nb, rows // tm), in_specs=in_specs, out_specs=out_specs, out_shape=out_shape,
        scratch_shapes=[pltpu.VMEM((halo + tm, d), F32)],
        compiler_params=_cparams("arbitrary", "arbitrary"), name="pre_rwkv",
    )(*ins)


def _wkv_body(*refs, L, has_state, mm_dtype):
    (r_ref, wl_ref, k_ref, v_ref, a_ref, g_ref, kk_ref, ka_ref, rk_ref, lng_ref, lnb_ref) = refs[:11]
    rest = refs[11:]
    if has_state:
        s0_ref, rest = rest[0], rest[1:]
    mo_ref, sout_ref, g_sc = rest
    c = pl.program_id(1)
    n_pairs = g_sc.shape[0]
    hb = HEAD_B
    mm = lambda x: x.astype(mm_dtype)

    @pl.when(c == 0)
    def _():
        if has_state:
            zero = jnp.zeros((hb, hb), F32)
            for pr in range(n_pairs):
                top = jnp.concatenate([s0_ref[2 * pr], zero], axis=1)
                bot = jnp.concatenate([zero, s0_ref[2 * pr + 1]], axis=1)
                g_sc[pr] = jnp.concatenate([top, bot], axis=0)
        else:
            g_sc[...] = jnp.zeros(g_sc.shape, F32)

    wl = wl_ref[...]
    ti = lax.broadcasted_iota(jnp.int32, (L, L), 0)
    tj = lax.broadcasted_iota(jnp.int32, (L, L), 1)
    tri = jnp.where(tj <= ti, 1.0, 0.0).astype(F32)
    cum = jnp.dot(tri, wl, preferred_element_type=F32, precision=lax.Precision.HIGHEST)
    gam = jnp.exp(cum)
    gam_prev = jnp.exp(cum - wl)
    inv_gam = jnp.exp(-cum)
    gam_last = gam[L - 1:L]

    lane = lax.broadcasted_iota(jnp.int32, (L, LANES), 1)
    in_h0 = lane < hb
    bi = lax.broadcasted_iota(jnp.int32, (LANES, LANES), 0)
    bj = lax.broadcasted_iota(jnp.int32, (LANES, LANES), 1)
    same_head = (bi < hb) == (bj < hb)
    ones_bd = jnp.where(same_head, 1.0, 0.0).astype(BF16)

    def headsum(x):
        hi = x.astype(BF16)
        lo = (x - hi.astype(F32)).astype(BF16)
        return _dot(hi, ones_bd) + _dot(lo, ones_bd)

    def stack(x):
        zero = jnp.zeros_like(x)
        return jnp.concatenate([jnp.where(in_h0, x, zero), jnp.where(in_h0, zero, x)], axis=0)

    wr = lax.broadcasted_iota(jnp.int32, (L, 2 * L), 0)
    wc = lax.broadcasted_iota(jnp.int32, (L, 2 * L), 1)
    wt = jnp.where(wc >= L, wc - L, wc)
    strict = wt < wr
    incl = wt <= wr
    w_h0 = wc < L

    def bd(w):
        zero = jnp.zeros_like(w)
        return jnp.concatenate([jnp.where(w_h0, w, zero), jnp.where(w_h0, zero, w)], axis=0)

    r_all, k_all, v_all, a_all, g_all = r_ref[...], k_ref[...], v_ref[...], a_ref[...], g_ref[...]
    for pr in range(n_pairs):
        sl = slice(LANES * pr, LANES * (pr + 1))
        a = a_all[:, sl]
        kraw = k_all[:, sl]
        v = v_all[:, sl]
        r = r_all[:, sl]
        kk = kraw * kk_ref[:, sl]
        kk = kk * lax.rsqrt(jnp.maximum(headsum(kk * kk), 1e-24))
        kmod = kraw * (1.0 + (a - 1.0) * ka_ref[:, sl])
        P = -kk * gam_prev[:, sl]
        Q = kk * a * inv_gam[:, sl]
        K = kmod * inv_gam[:, sl]
        R = r * gam[:, sl]

        qk4 = jnp.concatenate([stack(Q), stack(K)], axis=0)
        z = _dot_nt(mm(jnp.concatenate([P, R], axis=0)), mm(qk4))
        zero_w = jnp.zeros((L, 2 * L), F32)
        a_pq = jnp.where(strict, z[0:L, 0:2 * L], zero_w)
        a_pk = jnp.where(strict, z[0:L, 2 * L:4 * L], zero_w)
        a_rq = jnp.where(incl, z[L:2 * L, 0:2 * L], zero_w)
        a_rk = jnp.where(incl, z[L:2 * L, 2 * L:4 * L], zero_w)

        nmat = a_pq
        pw = a_pq
        for _ in range(int(math.log2(L)) - 1):
            pbd = mm(bd(pw))
            pw = _dot(mm(pw), pbd)
            nmat = nmat + pw + _dot(mm(nmat), mm(bd(pw)))

        gmat = g_sc[pr]
        gb = mm(gmat)
        sv = mm(stack(v))
        rhs = _dot_nt(mm(P), gb) + _dot(mm(a_pk), sv)
        u = rhs + _dot(mm(nmat), mm(stack(rhs)))
        y = _dot_nt(mm(R), gb) + _dot(mm(a_rq), mm(stack(u))) + _dot(mm(a_rk), sv)
        dg = _dot_tn(mm(jnp.concatenate([u, v], axis=0)), mm(jnp.concatenate([Q, K], axis=0)))
        gnew = (gmat + jnp.where(same_head, dg, jnp.zeros_like(dg))) * gam_last[:, sl]
        g_sc[pr] = gnew

        @pl.when(c == pl.num_programs(1) - 1)
        def _():
            sout_ref[2 * pr] = gnew[0:hb, 0:hb]
            sout_ref[2 * pr + 1] = gnew[hb:2 * hb, hb:2 * hb]

        mean = headsum(y) * (1.0 / hb)
        dlt = y - mean
        var = headsum(dlt * dlt) * (1.0 / hb)
        yn = dlt * lax.rsqrt(var + GN_EPS_B) * lng_ref[:, sl] + lnb_ref[:, sl]
        bonus = headsum(r * kmod * rk_ref[:, sl]) * v
        mo_ref[:, sl] = ((yn + bonus) * g_all[:, sl]).astype(BF16)


def _wkv(seqs, prm, state, L, time_major):
    row = lambda a: a.reshape(1, -1)
    d = prm['k_k'].shape[-1]
    n_heads = d // HEAD_B
    if time_major:
        t_len, width = seqs[0].shape
        nb = width // d
        seq_spec = pl.BlockSpec((L, d), lambda b, c: (c, b))
        mo_shape = jax.ShapeDtypeStruct((t_len, width), BF16)
    else:
        nb, t_len, _ = seqs[0].shape
        seq_spec = pl.BlockSpec((None, L, d), lambda b, c: (b, c, 0))
        mo_shape = jax.ShapeDtypeStruct((nb, t_len, d), BF16)
    has_state = state is not None
    st_spec = pl.BlockSpec((None, n_heads, HEAD_B, HEAD_B), lambda b, c: (b, 0, 0, 0))
    ins = list(seqs) + [row(prm['k_k']), row(prm['k_a']), row(prm['r_k']), row(prm['ln_g']), row(prm['ln_b'])]
    in_specs = [seq_spec] * 6 + [_resident((1, d))] * 5
    if has_state:
        ins.append(state)
        in_specs.append(st_spec)
    kern = functools.partial(_wkv_body, L=L, has_state=has_state, mm_dtype=BF16 if L >= 16 else F32)
    return pl.pallas_call(
        kern, grid=(nb, t_len // L), in_specs=in_specs,
        out_specs=[seq_spec, st_spec],
        out_shape=[mo_shape, jax.ShapeDtypeStruct((nb, n_heads, HEAD_B, HEAD_B), F32)],
        scratch_shapes=[pltpu.VMEM((n_heads // 2, LANES, LANES), F32)],
        compiler_params=_cparams("arbitrary", "arbitrary"), name="wkv",
    )(*ins)


def _layernorm_silu(y, g, b):
    mu = jnp.mean(y, axis=-1, keepdims=True)
    dlt = y - mu
    var = jnp.mean(dlt * dlt, axis=-1, keepdims=True)
    return _silu(dlt * lax.rsqrt(var + LN_EPS) * g + b)


def _pre_c_prompt_body(x_ref, sh_ref, sc_ref, gain_ref, w1, b1, wdw, bdw, lng, lnb,
                       mo_ref, cache_ref, ext, *, halo):
    j = pl.program_id(1)
    tm, d = x_ref.shape

    @pl.when(j == 0)
    def _():
        ext[0:halo, :] = jnp.zeros((halo, d), F32)

    h = _modnorm(x_ref[...], gain_ref[...], sh_ref[...], sc_ref[...]).astype(BF16)
    u = _dot(h, w1[...]) + b1[...]
    glu = u[:, :d] * _sigmoid(u[:, d:])
    ext[halo:halo + tm, :] = glu
    acc = glu * wdw[CONV_W - 1:CONV_W, :]
    for dd in range(1, CONV_W):
        acc = acc + ext[pl.ds(halo - dd, tm), :] * wdw[CONV_W - 1 - dd:CONV_W - dd, :]
    tail = ext[tm:tm + halo, :]
    ext[0:halo, :] = tail
    cache_ref[...] = tail
    mo_ref[...] = _layernorm_silu(acc + bdw[...], lng[...], lnb[...]).astype(BF16)


def _pre_c_sample_body(x_ref, sh_ref, sc_ref, gain_ref, w1, b1, wdw, bdw, lng, lnb, st_ref,
                       mo_ref, glu_ref, *, nb):
    tm, d = x_ref.shape
    n_t = tm // nb
    n_st = CONV_W - 1
    h = _modnorm(x_ref[...], gain_ref[...], sh_ref[...], sc_ref[...]).astype(BF16)
    u = _dot(h, w1[...]) + b1[...]
    glu = u[:, :d] * _sigmoid(u[:, d:])
    glu_ref[...] = glu
    for t in range(n_t):
        acc = None
        for w in range(CONV_W):
            pos = t + w
            if pos < n_st:
                slab = st_ref[nb * pos:nb * (pos + 1), :]
            else:
                slab = glu[nb * (pos - n_st):nb * (pos - n_st + 1), :]
            term = slab * wdw[w:w + 1, :]
            acc = term if acc is None else acc + term
        mo_ref[nb * t:nb * (t + 1), :] = _layernorm_silu(acc + bdw[...], lng[...], lnb[...]).astype(BF16)


def _pre_c(grp, layer, x, gain, prm, state, tm):
    nb, rows, d = x.shape
    row = lambda a: a.reshape(1, -1)
    ins = [x, grp.mod, grp.mod, gain, prm['w_pw1'], row(prm['b_pw1']), prm['w_dw'], row(prm['b_dw']),
           row(prm['ln_g']), row(prm['ln_b'])]
    in_specs = [grp.row_spec(tm, d), grp.mod_spec(layer, 0, d), grp.mod_spec(layer, 1, d)]
    in_specs += [_resident(a.shape) for a in ins[3:]]
    if grp.has_state:
        db = grp.sd
        ins.append(state)
        in_specs.append(_resident(state.shape))
        return pl.pallas_call(
            functools.partial(_pre_c_sample_body, nb=db), grid=(nb, 1), in_specs=in_specs,
            out_specs=[grp.row_spec(rows, d), grp.row_spec(rows, d)],
            out_shape=[jax.ShapeDtypeStruct((nb, rows, d), BF16), jax.ShapeDtypeStruct((nb, rows, d), F32)],
            compiler_params=_cparams("arbitrary", "arbitrary"), name="pre_conv_sample",
        )(*ins)
    halo = 32
    return pl.pallas_call(
        functools.partial(_pre_c_prompt_body, halo=halo), grid=(nb, rows // tm), in_specs=in_specs,
        out_specs=[grp.row_spec(tm, d), pl.BlockSpec((None, halo, d), lambda b, j: (b, 0, 0))],
        out_shape=[jax.ShapeDtypeStruct((nb, rows, d), BF16), jax.ShapeDtypeStruct((nb, halo, d), F32)],
        scratch_shapes=[pltpu.VMEM((halo + tm, d), F32)],
        compiler_params=_cparams("arbitrary", "arbitrary"), name="pre_conv_prompt",
    )(*ins)


def _post_ffn_body(*refs, sd, halo, has_state, final):
    (x_ref, mo_ref, g1_ref, sh_ref, sc_ref, g2_ref, wo, bo, gain_ref, w_in, wdw, bdw, w_out,
     nout_ref) = refs[:14]
    rest = refs[14:]
    if has_state:
        st_ref, rest = rest[0], rest[1:]
    y_ref, cache_ref, ext = rest
    j = pl.program_id(1)
    tm = x_ref.shape[0]
    dff = ext.shape[1]

    @pl.when(j == 0)
    def _():
        if has_state:
            ext[0:halo, :] = st_ref[...]
        else:
            ext[0:halo, :] = jnp.zeros((halo, dff), F32)

    x1 = x_ref[...] + _bc_rows(g1_ref[...], tm) * (_dot(mo_ref[...], wo[...]) + bo[...])
    h = _modnorm(x1, gain_ref[...], sh_ref[...], sc_ref[...]).astype(BF16)
    u = _dot(h, w_in[...])
    act = u[:, :dff]
    val = u[:, dff:]
    ext[halo:halo + tm, :] = act
    conv = (act * wdw[2:3, :] + ext[pl.ds(halo - sd, tm), :] * wdw[1:2, :]
            + ext[pl.ds(halo - 2 * sd, tm), :] * wdw[0:1, :] + bdw[...])
    tail = ext[tm:tm + halo, :]
    ext[0:halo, :] = tail
    cache_ref[...] = tail
    gated = (_silu(conv) * val).astype(BF16)
    x2 = x1 + _bc_rows(g2_ref[...], tm) * _dot(gated, w_out[...])
    if final:
        x2 = x2 * lax.rsqrt(jnp.mean(x2 * x2, axis=-1, keepdims=True) + RMS_EPS) * nout_ref[...]
    y_ref[...] = x2


def _post_ffn(grp, layer, x, mo, w_o, b_o, gain, prm, state, norm_out, tm, final):
    nb, rows, d = x.shape
    dff = prm['w_dw'].shape[1]
    halo = max(SUBLANES, (FFN_CONV_W - 1) * grp.sd)
    row = lambda a: a.reshape(1, -1)
    ins = [x, mo, grp.mod, grp.mod, grp.mod, grp.mod, w_o, row(b_o), gain, prm['w_in'], prm['w_dw'],
           row(prm['b_dw']), prm['w_out'], row(norm_out)]
    in_specs = [grp.row_spec(tm, d), grp.row_spec(tm, d), grp.mod_spec(layer, 2, d),
                grp.mod_spec(layer, 3, d), grp.mod_spec(layer, 4, d), grp.mod_spec(layer, 5, d)]
    in_specs += [_resident(a.shape) for a in ins[6:]]
    if grp.has_state:
        ins.append(state)
        in_specs.append(_resident(state.shape))
    kern = functools.partial(_post_ffn_body, sd=grp.sd, halo=halo, has_state=grp.has_state, final=final)
    return pl.pallas_call(
        kern, grid=(nb, rows // tm), in_specs=in_specs,
        out_specs=[grp.row_spec(tm, d), pl.BlockSpec((None, halo, dff), lambda b, j: (b, 0, 0))],
        out_shape=[jax.ShapeDtypeStruct((nb, rows, d), F32), jax.ShapeDtypeStruct((nb, halo, dff), F32)],
        scratch_shapes=[pltpu.VMEM((halo + tm, dff), F32)],
        compiler_params=_cparams("arbitrary", "arbitrary"), name="post_ffn",
    )(*ins)


def _trunk(grp, x, pos, P, states, paged, tiles):
    nb, rows, d = x.shape
    depth = P['norm_mix'].shape[0]
    out = dict(k=[], v=[], wkv=[], shift=[], conv=[], ffn=[])
    tabs = _rope_tables(pos)
    for i in range(depth):
        kind, j = i % N_MIXERS, i // N_MIXERS
        gain_mix = P['norm_mix'][i].reshape(1, d)
        zeros_d = jnp.zeros((d,), F32)
        if kind == 0:
            lam_init = 0.8 - 0.6 * math.exp(-0.3 * i)
            lvec = [P[n][j].reshape(1, DH_A) for n in ('a_lq1', 'a_lk1', 'a_lq2', 'a_lk2')]
            subln = P['a_subln'][j].reshape(1, DV_A)
            head_major = paged is None
            k, v, q, kh, vh = _pre_a(grp, i, x, gain_mix, P['a_w_qkv'][j], tabs, tiles['pre_a'], head_major)
            if paged is None:
                mo = _attn_prompt(q, kh, vh, *lvec, subln, lam_init, tiles['attn_q'])
            else:
                cache_k, cache_v, page_table = paged
                db = page_table.shape[0]
                t_new = rows // db
                tm_view = lambda a: a.reshape(t_new, db * a.shape[-1])
                mo = _attn_sample(tm_view(q), tm_view(k), tm_view(v), cache_k, cache_v, page_table, j,
                                  *lvec, subln, lam_init).reshape(nb, rows, d)
            out['k'].append(k)
            out['v'].append(v)
            w_o, b_o = P['a_w_o'][j], zeros_d
        elif kind == 1:
            prm = {n[2:]: P[n][j] for n in P if n.startswith('b_')}
            st_shift, st_wkv = (states['shift'][j], states['wkv'][j]) if grp.has_state else (None, None)
            r, wl, k, v, a, g, hl = _pre_b(grp, i, x, gain_mix, prm, st_shift, tiles['pre_b'])
            seqs = (r, wl, k, v, a, g)
            if grp.has_state:
                t_new = rows // grp.sd
                seqs = tuple(s.reshape(t_new, grp.sd * d) for s in seqs)
                mo, wkv = _wkv(seqs, prm, st_wkv, t_new, True)
                mo = mo.reshape(nb, rows, d)
            else:
                mo, wkv = _wkv(seqs, prm, None, tiles['wkv'], False)
            out['shift'].append(hl)
            out['wkv'].append(wkv)
            w_o, b_o = prm['w_o'], zeros_d
        else:
            prm = {n[3:]: P[n][j] for n in P if n.startswith('cv_')}
            st = states['conv'][j] if grp.has_state else None
            mo, cache = _pre_c(grp, i, x, gain_mix, prm, st, tiles['pre_c'])
            out['conv'].append(cache)
            w_o, b_o = prm['w_pw2'], prm['b_pw2']
        prm = {n[2:]: P[n][i] for n in ('f_w_in', 'f_w_dw', 'f_b_dw', 'f_w_out')}
        st = states['ffn'][i] if grp.has_state else None
        x, fcache = _post_ffn(grp, i, x, mo, w_o, b_o, P['norm_ffn'][i].reshape(1, d), prm, st,
                              P['norm_out'], tiles['ffn'], i == depth - 1)
        out['ffn'].append(fcache)
    return x, out


def kernel(x_prompt, x_sample, cache_k, cache_v, state_wkv, state_shift, state_conv, state_ffn, page_table, c_prompt, c_sample, norm_mix, norm_ffn, w_ada, b_ada, a_w_qkv, a_w_o, a_lq1, a_lk1, a_lq2, a_lk2, a_subln, b_mu, b_w_r, b_w_k, b_w_v, b_w_o, b_w0, b_w1, b_w2, b_a0, b_a1, b_a2, b_g1, b_g2, b_k_k, b_k_a, b_r_k, b_ln_g, b_ln_b, cv_w_pw1, cv_b_pw1, cv_w_dw, cv_b_dw, cv_ln_g, cv_ln_b, cv_w_pw2, cv_b_pw2, f_w_in, f_w_dw, f_b_dw, f_w_out, norm_out):
    bsz, seq, d = x_prompt.shape
    db, t_new, _ = x_sample.shape
    depth = norm_mix.shape[0]
    past = page_table.shape[1] * PAGE_SIZE
    bf = lambda a: a.astype(BF16)
    P = dict(norm_mix=norm_mix, norm_ffn=norm_ffn, norm_out=norm_out,
             a_w_qkv=bf(a_w_qkv), a_w_o=bf(a_w_o), a_lq1=a_lq1, a_lk1=a_lk1, a_lq2=a_lq2, a_lk2=a_lk2,
             a_subln=a_subln, b_mu=b_mu, b_w_r=bf(b_w_r), b_w_k=bf(b_w_k), b_w_v=bf(b_w_v),
             b_w_o=bf(b_w_o), b_w0=b_w0, b_w1=bf(b_w1), b_w2=bf(b_w2), b_a0=b_a0, b_a1=bf(b_a1),
             b_a2=bf(b_a2), b_g1=bf(b_g1), b_g2=bf(b_g2), b_k_k=b_k_k, b_k_a=b_k_a, b_r_k=b_r_k,
             b_ln_g=b_ln_g, b_ln_b=b_ln_b, cv_w_pw1=bf(cv_w_pw1), cv_b_pw1=cv_b_pw1, cv_w_dw=cv_w_dw,
             cv_b_dw=cv_b_dw, cv_ln_g=cv_ln_g, cv_ln_b=cv_ln_b, cv_w_pw2=bf(cv_w_pw2),
             cv_b_pw2=cv_b_pw2, f_w_in=bf(f_w_in), f_w_dw=f_w_dw, f_b_dw=f_b_dw, f_w_out=bf(f_w_out))

    pad = (-bsz) % SUBLANES
    c_all = jnp.concatenate([c_prompt, jnp.zeros((pad, d), F32), c_sample], axis=0)
    mod = _adaln(c_all, w_ada, b_ada)
    mod_p = mod[:, :bsz].reshape(depth, bsz, 1, 6 * d)
    mod_s = mod[:, bsz + pad:]

    grp_p = _Group(bsz, seq, 1, mod_p, False)
    tiles_p = dict(pre_a=min(512, seq), attn_q=min(512, seq), pre_b=min(512, seq), wkv=64,
                   pre_c=min(256, seq), ffn=min(256, seq))
    y_p, o_p = _trunk(grp_p, x_prompt, jnp.arange(seq), P, None, None, tiles_p)

    rows_s = t_new * db
    to_tm = lambda a: jnp.moveaxis(a, 1, 0)
    x_s = to_tm(x_sample).reshape(1, rows_s, d)
    states = dict(
        shift=[state_shift[j] for j in range(state_shift.shape[0])],
        wkv=[state_wkv[j] for j in range(state_wkv.shape[0])],
        conv=[to_tm(state_conv[j]).reshape(-1, d) for j in range(state_conv.shape[0])],
        ffn=[to_tm(state_ffn[i]).reshape(-1, state_ffn.shape[-1]) for i in range(depth)])
    grp_s = _Group(1, rows_s, db, mod_s, True)
    tiles_s = dict(pre_a=rows_s, pre_b=rows_s // 2, pre_c=rows_s, ffn=2 * db)
    pos_s = jnp.repeat(past + jnp.arange(t_new), db)
    y_s, o_s = _trunk(grp_s, x_s, pos_s, P, states, (cache_k, cache_v, page_table), tiles_s)

    from_tm = lambda a, w: jnp.moveaxis(a.reshape(w, db, a.shape[-1]), 0, 1)
    dff = f_w_dw.shape[-1]
    n_keep = CONV_W - 1
    outs = (
        y_p,
        from_tm(y_s, t_new),
        jnp.stack(o_p['k']).reshape(-1, bsz, seq, 2 * H_A, DH_A),
        jnp.stack(o_p['v']).reshape(-1, bsz, seq, H_A, DV_A),
        jnp.stack(o_p['wkv']),
        jnp.stack([h[:, -1] for h in o_p['shift']]),
        jnp.stack([c[:, -n_keep:] for c in o_p['conv']]),
        jnp.stack([f[:, -(FFN_CONV_W - 1):] for f in o_p['ffn']]),
        jnp.stack([from_tm(k, t_new) for k in o_s['k']]).reshape(-1, db, t_new, 2 * H_A, DH_A),
        jnp.stack([from_tm(v, t_new) for v in o_s['v']]).reshape(-1, db, t_new, H_A, DV_A),
        jnp.stack(o_s['wkv']),
        jnp.stack([h[0] for h in o_s['shift']]),
        jnp.stack([from_tm(jnp.concatenate([st.reshape(n_keep, db, d), glu.reshape(t_new, db, d)], axis=0)
                           [-n_keep:].reshape(-1, d), n_keep)
                   for st, glu in zip(states['conv'], o_s['conv'])]),
        jnp.stack([from_tm(f, FFN_CONV_W - 1) for f in o_s['ffn']]),
    )
    return outs
```

```python
import functools
import math

import numpy as np
import jax
import jax.numpy as jnp
from jax import lax
from jax.experimental import pallas as pl
from jax.experimental.pallas import tpu as pltpu

F32 = jnp.float32
BF16 = jnp.bfloat16

N_MIXERS = 3
H_A = 8
DH_A = 64
DV_A = 128
ROPE_THETA = 10000.0
PAGE_SIZE = 128
NEG_INF = -1e30
HEAD_B = 64
GN_EPS_B = 64e-5
LN_EPS = 1e-5
RMS_EPS = 1e-6
CONV_W = 31
FFN_CONV_W = 3

SUBLANES = 8
LANES = 128
VMEM_LIMIT_BYTES = 56 * 1024 * 1024

ATTN_RC = 128
ATTN_KC = 256
ATTN_DEPTH = 2


def _cparams(*sem):
    return pltpu.CompilerParams(dimension_semantics=sem, vmem_limit_bytes=VMEM_LIMIT_BYTES)


def _resident(shape):
    nd = len(shape)
    return pl.BlockSpec(shape, lambda *_: (0,) * nd, pipeline_mode=pl.Buffered(1))


def _dot(a, b):
    return jnp.dot(a, b, preferred_element_type=F32)


def _dot_nt(a, b):
    return lax.dot_general(a, b, (((1,), (1,)), ((), ())), preferred_element_type=F32)


def _dot_tn(a, b):
    return lax.dot_general(a, b, (((0,), (0,)), ((), ())), preferred_element_type=F32)


def _sigmoid(x):
    return 1.0 / (1.0 + jnp.exp(-x))


def _silu(x):
    return x * _sigmoid(x)


def _bc_rows(m, tm):
    nb, c = m.shape
    if nb == 1 or nb == tm:
        return m
    return jnp.broadcast_to(m[None], (tm // nb, nb, c)).reshape(tm, c)


def _modnorm(x, gain, shift, scale):
    tm = x.shape[0]
    y = x * lax.rsqrt(jnp.mean(x * x, axis=-1, keepdims=True) + RMS_EPS) * gain
    return y * (1.0 + _bc_rows(scale, tm)) + _bc_rows(shift, tm)


class _Group:
    def __init__(self, nb, rows, sd, mod, has_state):
        self.nb = nb
        self.rows = rows
        self.sd = sd
        self.mod = mod
        self.has_state = has_state

    def mod_spec(self, layer, k, d):
        if self.mod.ndim == 4:
            return pl.BlockSpec((None, None, 1, d), lambda b, j: (layer, b, 0, k))
        return pl.BlockSpec((None, self.mod.shape[1], d), lambda b, j: (layer, 0, k))

    def row_spec(self, tm, c):
        return pl.BlockSpec((None, tm, c), lambda b, j: (b, j, 0))


def _adaln_body(c_ref, w_ref, b_ref, o_ref):
    sc = _silu(c_ref[...]).astype(BF16)
    o_ref[...] = _dot(sc, w_ref[...].astype(BF16)) + b_ref[...]


def _adaln(c_all, w_ada, b_ada):
    depth, d, n = w_ada.shape
    bp = c_all.shape[0]
    tn = 1536
    return pl.pallas_call(
        _adaln_body,
        grid=(depth, n // tn),
        in_specs=[pl.BlockSpec((bp, d), lambda i, j: (0, 0)),
                  pl.BlockSpec((None, d, tn), lambda i, j: (i, 0, j)),
                  pl.BlockSpec((None, 1, tn), lambda i, j: (i, 0, j))],
        out_specs=pl.BlockSpec((None, bp, tn), lambda i, j: (i, 0, j)),
        out_shape=jax.ShapeDtypeStruct((depth, bp, n), F32),
        compiler_params=_cparams("arbitrary", "arbitrary"),
        name="adaln",
    )(c_all, w_ada, b_ada.reshape(depth, 1, n))


def _pre_a_body(x_ref, sh_ref, sc_ref, gain_ref, w_ref, cos_ref, sina_ref, sinb_ref,
                k_out, v_out, q_out, kh_out, vh_out, *, head_major, q_scale):
    h = _modnorm(x_ref[...], gain_ref[...], sh_ref[...], sc_ref[...]).astype(BF16)
    qkv = _dot(h, w_ref[...])
    cos, sina, sinb = cos_ref[...], sina_ref[...], sinb_ref[...]
    nqk = H_A * LANES
    for s in range(2 * H_A):
        xs = qkv[:, LANES * s:LANES * (s + 1)]
        ro = xs * cos + pltpu.roll(xs, LANES - 32, 1) * sina + pltpu.roll(xs, 32, 1) * sinb
        if s < H_A:
            if head_major:
                q_out[s] = (ro * q_scale).astype(BF16)
            else:
                q_out[:, LANES * s:LANES * (s + 1)] = ro * q_scale
        else:
            hh = s - H_A
            k_out[:, LANES * hh:LANES * (hh + 1)] = ro
            if head_major:
                kh_out[hh] = ro.astype(BF16)
    v = qkv[:, 2 * nqk:]
    v_out[...] = v
    if head_major:
        for hh in range(H_A):
            vh_out[hh] = v[:, LANES * hh:LANES * (hh + 1)].astype(BF16)


def _rope_tables(pos):
    half = DH_A // 2
    inv = jnp.exp(-math.log(ROPE_THETA) * jnp.arange(half, dtype=F32) * (2.0 / DH_A))
    ang = pos.astype(F32)[:, None] * inv[None, :]
    cos, sin = jnp.cos(ang), jnp.sin(ang)
    zero = jnp.zeros_like(sin)
    cos_t = jnp.concatenate([cos, cos, cos, cos], axis=1)
    sina = jnp.concatenate([-sin, zero, -sin, zero], axis=1)
    sinb = jnp.concatenate([zero, sin, zero, sin], axis=1)
    return cos_t, sina, sinb


def _pre_a(grp, layer, x, gain, w_qkv, tabs, tm, head_major):
    nb, rows, d = x.shape
    n = w_qkv.shape[1]
    dk = H_A * LANES
    q_scale = DH_A ** -0.5 * (math.log2(math.e) if head_major else 1.0)
    kern = functools.partial(_pre_a_body, head_major=head_major, q_scale=q_scale)
    tab_spec = pl.BlockSpec((tm, LANES), lambda b, j: (j, 0))
    in_specs = [grp.row_spec(tm, d), grp.mod_spec(layer, 0, d), grp.mod_spec(layer, 1, d),
                _resident((1, d)), _resident((d, n)), tab_spec, tab_spec, tab_spec]
    out_specs = [grp.row_spec(tm, dk), grp.row_spec(tm, dk)]
    out_shape = [jax.ShapeDtypeStruct((nb, rows, dk), F32), jax.ShapeDtypeStruct((nb, rows, dk), F32)]
    if head_major:
        hspec = pl.BlockSpec((None, H_A, tm, LANES), lambda b, j: (b, 0, j, 0))
        out_specs += [hspec, hspec, hspec]
        out_shape += [jax.ShapeDtypeStruct((nb, H_A, rows, LANES), BF16)] * 3
    else:
        dummy = pl.BlockSpec((None, SUBLANES, LANES), lambda b, j: (b, 0, 0))
        out_specs += [grp.row_spec(tm, dk), dummy, dummy]
        out_shape += [jax.ShapeDtypeStruct((nb, rows, dk), F32),
                      jax.ShapeDtypeStruct((nb, SUBLANES, LANES), BF16),
                      jax.ShapeDtypeStruct((nb, SUBLANES, LANES), BF16)]
    return pl.pallas_call(
        kern, grid=(nb, rows // tm), in_specs=in_specs, out_specs=out_specs, out_shape=out_shape,
        compiler_params=_cparams("arbitrary", "arbitrary"), name="pre_attn",
    )(x, grp.mod, grp.mod, gain, w_qkv, *tabs)


def _lam(lq1, lk1, lq2, lk2, lam_init):
    return (jnp.exp(jnp.sum(lq1 * lk1, axis=-1, keepdims=True))
            - jnp.exp(jnp.sum(lq2 * lk2, axis=-1, keepdims=True)) + lam_init)


def _diff_finish(acc, l, lam, g, lam_init, n):
    o = acc[0:n] / l[0:n] - lam * (acc[n:2 * n] / l[n:2 * n])
    return o * lax.rsqrt(jnp.mean(o * o, axis=-1, keepdims=True) + RMS_EPS) * g * (1.0 - lam_init)


def _attn_p_body(qi_tab, ki_tab, q_ref, k_ref, v_ref, lq1, lk1, lq2, lk2, g_ref, o_ref,
                 qs_sc, m_sc, l_sc, acc_sc, *, tq, rc, kc, depth, lam_init):
    t = pl.program_id(2)
    qi = qi_tab[t]
    ki = ki_tab[t]
    n_lane_tiles = kc // LANES

    @pl.when(ki == 0)
    def _():
        q = q_ref[...]
        lane = lax.broadcasted_iota(jnp.int32, q.shape, 1)
        zero = jnp.zeros_like(q)
        qs_sc[0:tq, :] = jnp.where(lane < DH_A, q, zero)
        qs_sc[tq:2 * tq, :] = jnp.where(lane >= DH_A, q, zero)
        m_sc[...] = jnp.full(m_sc.shape, NEG_INF, F32)
        l_sc[...] = jnp.zeros(l_sc.shape, F32)
        acc_sc[...] = jnp.zeros(acc_sc.shape, F32)

    def scores(blk):
        r0, k0, masked = blk
        s = _dot_nt(qs_sc[r0:r0 + rc, :], k_ref[k0:k0 + kc, :])
        if masked:
            qpos = lax.broadcasted_iota(jnp.int32, s.shape, 0) + (r0 % tq)
            kpos = lax.broadcasted_iota(jnp.int32, s.shape, 1) + k0
            s = jnp.where(kpos <= qpos, s, NEG_INF)
        return s

    def softmax_pv(blk, s):
        r0, k0, _ = blk
        rows = slice(r0, r0 + rc)
        m_prev = m_sc[rows, :]
        m_new = jnp.maximum(m_prev, jnp.max(s, axis=1, keepdims=True))
        alpha = jnp.exp2(m_prev - m_new)
        p = jnp.exp2(s - jnp.concatenate([m_new] * n_lane_tiles, axis=1))
        psum = p[:, 0:LANES]
        for i in range(1, n_lane_tiles):
            psum = psum + p[:, LANES * i:LANES * (i + 1)]
        m_sc[rows, :] = m_new
        l_sc[rows, :] = alpha * l_sc[rows, :] + psum
        acc_sc[rows, :] = alpha * acc_sc[rows, :] + _dot(p.astype(BF16), v_ref[k0:k0 + kc, :])

    def run(blocks):
        pending = []
        for blk in blocks:
            pending.append((blk, scores(blk)))
            if len(pending) > depth:
                softmax_pv(*pending.pop(0))
        for item in pending:
            softmax_pv(*item)

    @pl.when(ki < qi)
    def _():
        run([(r0, k0, False) for k0 in range(0, tq, kc) for r0 in range(0, 2 * tq, rc)])

    @pl.when(ki == qi)
    def _():
        blocks = []
        for k0 in range(0, tq, kc):
            for r0 in range(0, 2 * tq, rc):
                q_lo = r0 % tq
                if k0 <= q_lo + rc - 1:
                    blocks.append((r0, k0, k0 + kc - 1 > q_lo))
        run(blocks)
        lam = _lam(lq1[...], lk1[...], lq2[...], lk2[...], lam_init)
        l_tot = jnp.sum(l_sc[...], axis=1, keepdims=True)
        o_ref[...] = _diff_finish(acc_sc[...], l_tot, lam, g_ref[...], lam_init, tq).astype(BF16)


def _attn_prompt(qh, kh, vh, lq1, lk1, lq2, lk2, g, lam_init, tq):
    nb, nh, s, _ = qh.shape
    nq = s // tq
    rc = min(ATTN_RC, tq)
    kc = min(ATTN_KC, tq)
    pairs = [(qi, ki) for qi in range(nq) for ki in range(qi + 1)]
    qi_tab = jnp.asarray(np.array([p[0] for p in pairs], np.int32))
    ki_tab = jnp.asarray(np.array([p[1] for p in pairs], np.int32))
    small = pl.BlockSpec((1, DH_A), lambda b, h, t, qt, kt: (0, 0))
    grid_spec = pltpu.PrefetchScalarGridSpec(
        num_scalar_prefetch=2,
        grid=(nb, nh, len(pairs)),
        in_specs=[pl.BlockSpec((None, None, tq, LANES), lambda b, h, t, qt, kt: (b, h, qt[t], 0)),
                  pl.BlockSpec((None, None, tq, LANES), lambda b, h, t, qt, kt: (b, h, kt[t], 0)),
                  pl.BlockSpec((None, None, tq, LANES), lambda b, h, t, qt, kt: (b, h, kt[t], 0)),
                  small, small, small, small,
                  pl.BlockSpec((1, DV_A), lambda b, h, t, qt, kt: (0, 0))],
        out_specs=pl.BlockSpec((None, tq, LANES), lambda b, h, t, qt, kt: (b, qt[t], h)),
        scratch_shapes=[pltpu.VMEM((2 * tq, LANES), BF16), pltpu.VMEM((2 * tq, LANES), F32),
                        pltpu.VMEM((2 * tq, LANES), F32), pltpu.VMEM((2 * tq, DV_A), F32)])
    return pl.pallas_call(
        functools.partial(_attn_p_body, tq=tq, rc=rc, kc=kc, depth=ATTN_DEPTH, lam_init=lam_init),
        grid_spec=grid_spec,
        out_shape=jax.ShapeDtypeStruct((nb, s, nh * LANES), BF16),
        compiler_params=_cparams("arbitrary", "arbitrary", "arbitrary"), name="attn_prompt",
    )(qi_tab, ki_tab, qh, kh, vh, lq1, lk1, lq2, lk2, g)


def _attn_s_body(pt, q_ref, kn_ref, vn_ref, *rest, n_pages, lam_init):
    k_refs = rest[:n_pages]
    v_refs = rest[n_pages:2 * n_pages]
    lq1, lk1, lq2, lk2, g_ref, o_ref = rest[2 * n_pages:]
    t_new, dk = q_ref.shape
    n_sub = 2 * H_A
    n_rows = n_sub * t_new
    page = k_refs[0].shape[1]

    q_tiled = jnp.concatenate([q_ref[...]] * n_sub, axis=0)
    ri = lax.broadcasted_iota(jnp.int32, (n_rows, dk), 0)
    ci = lax.broadcasted_iota(jnp.int32, (n_rows, dk), 1)
    same_sub = (lax.shift_right_logical(ri, int(math.log2(t_new)))
                == lax.shift_right_logical(ci, int(math.log2(DH_A))))
    qbd = jnp.where(same_sub, q_tiled, jnp.zeros_like(q_tiled)).astype(BF16)

    s_past = jnp.concatenate([_dot(qbd, k_refs[p][...].astype(BF16)) for p in range(n_pages)], axis=1)
    s_new = _dot_nt(qbd, kn_ref[...].astype(BF16))
    row = lax.broadcasted_iota(jnp.int32, s_new.shape, 0)
    col = lax.broadcasted_iota(jnp.int32, s_new.shape, 1)
    s_new = jnp.where(col <= jnp.bitwise_and(row, t_new - 1), s_new, NEG_INF)
    m = jnp.maximum(jnp.max(s_past, axis=1, keepdims=True), jnp.max(s_new, axis=1, keepdims=True))
    p_past = jnp.exp(s_past - m)
    p_new = jnp.exp(s_new - m)
    l = jnp.sum(p_past, axis=1, keepdims=True) + jnp.sum(p_new, axis=1, keepdims=True)
    p_past = p_past.astype(BF16)

    lam = _lam(lq1[...], lk1[...], lq2[...], lk2[...], lam_init)
    vn = vn_ref[...]
    for hh in range(H_A):
        rows = slice(2 * t_new * hh, 2 * t_new * (hh + 1))
        v_h = jnp.concatenate([v_refs[p][pl.ds(hh, page, stride=H_A), :] for p in range(n_pages)],
                              axis=0).astype(BF16)
        acc = _dot(p_past[rows, :], v_h) + _dot(p_new[rows, :], vn[:, LANES * hh:LANES * (hh + 1)])
        o = _diff_finish(acc, l[rows, :], lam, g_ref[...], lam_init, t_new)
        o_ref[:, LANES * hh:LANES * (hh + 1)] = o.astype(BF16)


def _attn_sample(q, k_new, v_new, cache_kt, cache_v2, page_table, j, lq1, lk1, lq2, lk2, g, lam_init):
    t_new, width = q.shape
    db, n_pages = page_table.shape
    dk = width // db
    small = pl.BlockSpec((1, DH_A), lambda b, pt: (0, 0))
    seq = pl.BlockSpec((t_new, dk), lambda b, pt: (0, b))

    def page_spec(arr, p):
        return pl.BlockSpec((None, None) + arr.shape[2:], lambda b, pt: (j, pt[b, p], 0, 0))

    grid_spec = pltpu.PrefetchScalarGridSpec(
        num_scalar_prefetch=1,
        grid=(db,),
        in_specs=([seq, seq, seq] + [page_spec(cache_kt, p) for p in range(n_pages)]
                  + [page_spec(cache_v2, p) for p in range(n_pages)]
                  + [small, small, small, small, pl.BlockSpec((1, DV_A), lambda b, pt: (0, 0))]),
        out_specs=seq)
    return pl.pallas_call(
        functools.partial(_attn_s_body, n_pages=n_pages, lam_init=lam_init),
        grid_spec=grid_spec,
        out_shape=jax.ShapeDtypeStruct((t_new, width), BF16),
        compiler_params=_cparams("arbitrary"), name="attn_sample",
    )(page_table, q, k_new, v_new, *([cache_kt] * n_pages), *([cache_v2] * n_pages),
      lq1, lk1, lq2, lk2, g)


def _pre_b_body(*refs, sd, halo, has_state):
    (x_ref, sh_ref, sc_ref, gain_ref, mu_ref, wr, wk, wv, w0, w1, w2, a0, a1, a2, g1, g2) = refs[:16]
    rest = refs[16:]
    if has_state:
        st_ref, rest = rest[0], rest[1:]
    r_out, wl_out, k_out, v_out, a_out, g_out, hl_out, ext = rest
    j = pl.program_id(1)
    tm = x_ref.shape[0]

    @pl.when(j == 0)
    def _():
        if has_state:
            ext[0:halo, :] = st_ref[...]
        else:
            ext[0:halo, :] = jnp.zeros((halo, ext.shape[1]), F32)

    h = _modnorm(x_ref[...], gain_ref[...], sh_ref[...], sc_ref[...])
    ext[halo:halo + tm, :] = h
    xx = ext[pl.ds(halo - sd, tm), :] - h
    ext[0:halo, :] = ext[tm:tm + halo, :]
    hl_out[...] = h[tm - halo:tm]

    mu = mu_ref[...]
    mix = lambda n: (h + xx * mu[n:n + 1]).astype(BF16)
    r_out[...] = _dot(mix(0), wr[...])
    wraw = w0[...] + _dot(jnp.tanh(_dot(mix(1), w1[...])).astype(BF16), w2[...])
    k_out[...] = _dot(mix(2), wk[...])
    v_out[...] = _dot(mix(3), wv[...])
    a_out[...] = _sigmoid(a0[...] + _dot(_dot(mix(4), a1[...]).astype(BF16), a2[...]))
    g_out[...] = _dot(_sigmoid(_dot(mix(5), g1[...])).astype(BF16), g2[...])
    z = -wraw
    softplus = jnp.maximum(z, 0.0) + jnp.log(1.0 + jnp.exp(-jnp.abs(z)))
    wl_out[...] = -jnp.exp(-softplus - 0.5)


def _pre_b(grp, layer, x, gain, prm, state, tm):
    nb, rows, d = x.shape
    halo = max(SUBLANES, grp.sd)
    kern = functools.partial(_pre_b_body, sd=grp.sd, halo=halo, has_state=grp.has_state)
    row = lambda a: a.reshape(1, -1)
    ins = [x, grp.mod, grp.mod, gain, prm['mu'], prm['w_r'], prm['w_k'], prm['w_v'], row(prm['w0']),
           prm['w1'], prm['w2'], row(prm['a0']), prm['a1'], prm['a2'], prm['g1'], prm['g2']]
    in_specs = [grp.row_spec(tm, d), grp.mod_spec(layer, 0, d), grp.mod_spec(layer, 1, d)]
    in_specs += [_resident(a.shape) for a in ins[3:]]
    if grp.has_state:
        ins.append(state)
        in_specs.append(_resident(state.shape))
    big = jax.ShapeDtypeStruct((nb, rows, d), F32)
    out_specs = [grp.row_spec(tm, d)] * 6 + [pl.BlockSpec((None, halo, d), lambda b, j: (b, 0, 0))]
    out_shape = [big] * 6 + [jax.ShapeDtypeStruct((nb, halo, d), F32)]
    return pl.pallas_call(
        kern, grid=(nb, rows // tm), in_specs=in_specs, out_specs=out_specs, out_shape=out_shape,
        scratch_shapes=[pltpu.VMEM((halo + tm, d), F32)],
        compiler_params=_cparams("arbitrary", "arbitrary"), name="pre_rwkv",
    )(*ins)


def _wkv_body(*refs, L, has_state, mm_dtype):
    (r_ref, wl_ref, k_ref, v_ref, a_ref, g_ref, kk_ref, ka_ref, rk_ref, lng_ref, lnb_ref) = refs[:11]
    rest = refs[11:]
    if has_state:
        s0_ref, rest = rest[0], rest[1:]
    mo_ref, sout_ref, g_sc = rest
    c = pl.program_id(1)
    n_pairs = g_sc.shape[0]
    hb = HEAD_B
    mm = lambda x: x.astype(mm_dtype)

    @pl.when(c == 0)
    def _():
        if has_state:
            zero = jnp.zeros((hb, hb), F32)
            for pr in range(n_pairs):
                top = jnp.concatenate([s0_ref[2 * pr], zero], axis=1)
                bot = jnp.concatenate([zero, s0_ref[2 * pr + 1]], axis=1)
                g_sc[pr] = jnp.concatenate([top, bot], axis=0)
        else:
            g_sc[...] = jnp.zeros(g_sc.shape, F32)

    wl = wl_ref[...]
    ti = lax.broadcasted_iota(jnp.int32, (L, L), 0)
    tj = lax.broadcasted_iota(jnp.int32, (L, L), 1)
    tri = jnp.where(tj <= ti, 1.0, 0.0).astype(F32)
    cum = jnp.dot(tri, wl, preferred_element_type=F32, precision=lax.Precision.HIGHEST)
    gam = jnp.exp(cum)
    gam_prev = jnp.exp(cum - wl)
    inv_gam = jnp.exp(-cum)
    gam_last = gam[L - 1:L]

    lane = lax.broadcasted_iota(jnp.int32, (L, LANES), 1)
    in_h0 = lane < hb
    bi = lax.broadcasted_iota(jnp.int32, (LANES, LANES), 0)
    bj = lax.broadcasted_iota(jnp.int32, (LANES, LANES), 1)
    same_head = (bi < hb) == (bj < hb)
    ones_bd = jnp.where(same_head, 1.0, 0.0).astype(BF16)

    def headsum(x):
        hi = x.astype(BF16)
        lo = (x - hi.astype(F32)).astype(BF16)
        return _dot(hi, ones_bd) + _dot(lo, ones_bd)

    def stack(x):
        zero = jnp.zeros_like(x)
        return jnp.concatenate([jnp.where(in_h0, x, zero), jnp.where(in_h0, zero, x)], axis=0)

    wr = lax.broadcasted_iota(jnp.int32, (L, 2 * L), 0)
    wc = lax.broadcasted_iota(jnp.int32, (L, 2 * L), 1)
    wt = jnp.where(wc >= L, wc - L, wc)
    strict = wt < wr
    incl = wt <= wr
    w_h0 = wc < L

    def bd(w):
        zero = jnp.zeros_like(w)
        return jnp.concatenate([jnp.where(w_h0, w, zero), jnp.where(w_h0, zero, w)], axis=0)

    r_all, k_all, v_all, a_all, g_all = r_ref[...], k_ref[...], v_ref[...], a_ref[...], g_ref[...]
    prs = range(n_pairs)
    sls = [slice(LANES * pr, LANES * (pr + 1)) for pr in prs]
    a = [a_all[:, sl] for sl in sls]
    kraw = [k_all[:, sl] for sl in sls]
    v = [v_all[:, sl] for sl in sls]
    r = [r_all[:, sl] for sl in sls]
    kk = [kraw[i] * kk_ref[:, sls[i]] for i in prs]
    kmod = [kraw[i] * (1.0 + (a[i] - 1.0) * ka_ref[:, sls[i]]) for i in prs]
    ssq = [headsum(kk[i] * kk[i]) for i in prs]
    bonus_w = [headsum(r[i] * kmod[i] * rk_ref[:, sls[i]]) for i in prs]
    kk = [kk[i] * lax.rsqrt(jnp.maximum(ssq[i], 1e-24)) for i in prs]
    P = [-kk[i] * gam_prev[:, sls[i]] for i in prs]
    Q = [kk[i] * a[i] * inv_gam[:, sls[i]] for i in prs]
    K = [kmod[i] * inv_gam[:, sls[i]] for i in prs]
    R = [r[i] * gam[:, sls[i]] for i in prs]

    z = [_dot_nt(mm(jnp.concatenate([P[i], R[i]], axis=0)),
                 mm(jnp.concatenate([stack(Q[i]), stack(K[i])], axis=0))) for i in prs]
    zero_w = jnp.zeros((L, 2 * L), F32)
    a_pq = [jnp.where(strict, z[i][0:L, 0:2 * L], zero_w) for i in prs]
    a_pk = [jnp.where(strict, z[i][0:L, 2 * L:4 * L], zero_w) for i in prs]
    a_rq = [jnp.where(incl, z[i][L:2 * L, 0:2 * L], zero_w) for i in prs]
    a_rk = [jnp.where(incl, z[i][L:2 * L, 2 * L:4 * L], zero_w) for i in prs]

    nmat = list(a_pq)
    pw = list(a_pq)
    for _ in range(int(math.log2(L)) - 1):
        pw = [_dot(mm(pw[i]), mm(bd(pw[i]))) for i in prs]
        nmat = [nmat[i] + pw[i] + _dot(mm(nmat[i]), mm(bd(pw[i]))) for i in prs]

    gmat = [g_sc[i] for i in prs]
    gb = [mm(gmat[i]) for i in prs]
    sv = [mm(stack(v[i])) for i in prs]
    rhs = [_dot_nt(mm(P[i]), gb[i]) + _dot(mm(a_pk[i]), sv[i]) for i in prs]
    u = [rhs[i] + _dot(mm(nmat[i]), mm(stack(rhs[i]))) for i in prs]
    y = [_dot_nt(mm(R[i]), gb[i]) + _dot(mm(a_rq[i]), mm(stack(u[i]))) + _dot(mm(a_rk[i]), sv[i])
         for i in prs]
    dg = [_dot_tn(mm(jnp.concatenate([u[i], v[i]], axis=0)), mm(jnp.concatenate([Q[i], K[i]], axis=0)))
          for i in prs]
    for i in prs:
        g_sc[i] = (gmat[i] + jnp.where(same_head, dg[i], jnp.zeros_like(dg[i]))) * gam_last[:, sls[i]]

    mean = [headsum(y[i]) * (1.0 / hb) for i in prs]
    dlt = [y[i] - mean[i] for i in prs]
    var = [headsum(dlt[i] * dlt[i]) * (1.0 / hb) for i in prs]
    for i in prs:
        yn = dlt[i] * lax.rsqrt(var[i] + GN_EPS_B) * lng_ref[:, sls[i]] + lnb_ref[:, sls[i]]
        mo_ref[:, sls[i]] = ((yn + bonus_w[i] * v[i]) * g_all[:, sls[i]]).astype(BF16)

    @pl.when(c == pl.num_programs(1) - 1)
    def _():
        for pr in range(n_pairs):
            gfin = g_sc[pr]
            sout_ref[2 * pr] = gfin[0:hb, 0:hb]
            sout_ref[2 * pr + 1] = gfin[hb:2 * hb, hb:2 * hb]


def _wkv(seqs, prm, state, L, time_major):
    row = lambda a: a.reshape(1, -1)
    d = prm['k_k'].shape[-1]
    n_heads = d // HEAD_B
    if time_major:
        t_len, width = seqs[0].shape
        nb = width // d
        seq_spec = pl.BlockSpec((L, d), lambda b, c: (c, b))
        mo_shape = jax.ShapeDtypeStruct((t_len, width), BF16)
    else:
        nb, t_len, _ = seqs[0].shape
        seq_spec = pl.BlockSpec((None, L, d), lambda b, c: (b, c, 0))
        mo_shape = jax.ShapeDtypeStruct((nb, t_len, d), BF16)
    has_state = state is not None
    st_spec = pl.BlockSpec((None, n_heads, HEAD_B, HEAD_B), lambda b, c: (b, 0, 0, 0))
    ins = list(seqs) + [row(prm['k_k']), row(prm['k_a']), row(prm['r_k']), row(prm['ln_g']), row(prm['ln_b'])]
    in_specs = [seq_spec] * 6 + [_resident((1, d))] * 5
    if has_state:
        ins.append(state)
        in_specs.append(st_spec)
    kern = functools.partial(_wkv_body, L=L, has_state=has_state, mm_dtype=BF16 if L >= 16 else F32)
    return pl.pallas_call(
        kern, grid=(nb, t_len // L), in_specs=in_specs,
        out_specs=[seq_spec, st_spec],
        out_shape=[mo_shape, jax.ShapeDtypeStruct((nb, n_heads, HEAD_B, HEAD_B), F32)],
        scratch_shapes=[pltpu.VMEM((n_heads // 2, LANES, LANES), F32)],
        compiler_params=_cparams("arbitrary", "arbitrary"), name="wkv",
    )(*ins)


def _layernorm_silu(y, g, b):
    mu = jnp.mean(y, axis=-1, keepdims=True)
    dlt = y - mu
    var = jnp.mean(dlt * dlt, axis=-1, keepdims=True)
    return _silu(dlt * lax.rsqrt(var + LN_EPS) * g + b)


def _pre_c_prompt_body(x_ref, sh_ref, sc_ref, gain_ref, w1, b1, wdw, bdw, lng, lnb,
                       mo_ref, cache_ref, ext, *, halo):
    j = pl.program_id(1)
    tm, d = x_ref.shape

    @pl.when(j == 0)
    def _():
        ext[0:halo, :] = jnp.zeros((halo, d), F32)

    h = _modnorm(x_ref[...], gain_ref[...], sh_ref[...], sc_ref[...]).astype(BF16)
    u = _dot(h, w1[...]) + b1[...]
    glu = u[:, :d] * _sigmoid(u[:, d:])
    ext[halo:halo + tm, :] = glu
    acc = glu * wdw[CONV_W - 1:CONV_W, :]
    for dd in range(1, CONV_W):
        acc = acc + ext[pl.ds(halo - dd, tm), :] * wdw[CONV_W - 1 - dd:CONV_W - dd, :]
    tail = ext[tm:tm + halo, :]
    ext[0:halo, :] = tail
    cache_ref[...] = tail
    mo_ref[...] = _layernorm_silu(acc + bdw[...], lng[...], lnb[...]).astype(BF16)


def _pre_c_sample_body(x_ref, sh_ref, sc_ref, gain_ref, w1, b1, wdw, bdw, lng, lnb, st_ref,
                       mo_ref, glu_ref, *, nb):
    tm, d = x_ref.shape
    n_t = tm // nb
    n_st = CONV_W - 1
    h = _modnorm(x_ref[...], gain_ref[...], sh_ref[...], sc_ref[...]).astype(BF16)
    u = _dot(h, w1[...]) + b1[...]
    glu = u[:, :d] * _sigmoid(u[:, d:])
    glu_ref[...] = glu
    for t in range(n_t):
        acc = None
        for w in range(CONV_W):
            pos = t + w
            if pos < n_st:
                slab = st_ref[nb * pos:nb * (pos + 1), :]
            else:
                slab = glu[nb * (pos - n_st):nb * (pos - n_st + 1), :]
            term = slab * wdw[w:w + 1, :]
            acc = term if acc is None else acc + term
        mo_ref[nb * t:nb * (t + 1), :] = _layernorm_silu(acc + bdw[...], lng[...], lnb[...]).astype(BF16)


def _pre_c(grp, layer, x, gain, prm, state, tm):
    nb, rows, d = x.shape
    row = lambda a: a.reshape(1, -1)
    ins = [x, grp.mod, grp.mod, gain, prm['w_pw1'], row(prm['b_pw1']), prm['w_dw'], row(prm['b_dw']),
           row(prm['ln_g']), row(prm['ln_b'])]
    in_specs = [grp.row_spec(tm, d), grp.mod_spec(layer, 0, d), grp.mod_spec(layer, 1, d)]
    in_specs += [_resident(a.shape) for a in ins[3:]]
    if grp.has_state:
        db = grp.sd
        ins.append(state)
        in_specs.append(_resident(state.shape))
        return pl.pallas_call(
            functools.partial(_pre_c_sample_body, nb=db), grid=(nb, 1), in_specs=in_specs,
            out_specs=[grp.row_spec(rows, d), grp.row_spec(rows, d)],
            out_shape=[jax.ShapeDtypeStruct((nb, rows, d), BF16), jax.ShapeDtypeStruct((nb, rows, d), F32)],
            compiler_params=_cparams("arbitrary", "arbitrary"), name="pre_conv_sample",
        )(*ins)
    halo = 32
    return pl.pallas_call(
        functools.partial(_pre_c_prompt_body, halo=halo), grid=(nb, rows // tm), in_specs=in_specs,
        out_specs=[grp.row_spec(tm, d), pl.BlockSpec((None, halo, d), lambda b, j: (b, 0, 0))],
        out_shape=[jax.ShapeDtypeStruct((nb, rows, d), BF16), jax.ShapeDtypeStruct((nb, halo, d), F32)],
        scratch_shapes=[pltpu.VMEM((halo + tm, d), F32)],
        compiler_params=_cparams("arbitrary", "arbitrary"), name="pre_conv_prompt",
    )(*ins)


def _post_ffn_body(*refs, sd, halo, has_state, final):
    (x_ref, mo_ref, g1_ref, sh_ref, sc_ref, g2_ref, wo, bo, gain_ref, w_in, wdw, bdw, w_out,
     nout_ref) = refs[:14]
    rest = refs[14:]
    if has_state:
        st_ref, rest = rest[0], rest[1:]
    y_ref, cache_ref, ext = rest
    j = pl.program_id(1)
    tm = x_ref.shape[0]
    dff = ext.shape[1]

    @pl.when(j == 0)
    def _():
        if has_state:
            ext[0:halo, :] = st_ref[...]
        else:
            ext[0:halo, :] = jnp.zeros((halo, dff), F32)

    x1 = x_ref[...] + _bc_rows(g1_ref[...], tm) * (_dot(mo_ref[...], wo[...]) + bo[...])
    h = _modnorm(x1, gain_ref[...], sh_ref[...], sc_ref[...]).astype(BF16)
    u = _dot(h, w_in[...])
    act = u[:, :dff]
    val = u[:, dff:]
    ext[halo:halo + tm, :] = act
    conv = (act * wdw[2:3, :] + ext[pl.ds(halo - sd, tm), :] * wdw[1:2, :]
            + ext[pl.ds(halo - 2 * sd, tm), :] * wdw[0:1, :] + bdw[...])
    tail = ext[tm:tm + halo, :]
    ext[0:halo, :] = tail
    cache_ref[...] = tail
    gated = (_silu(conv) * val).astype(BF16)
    x2 = x1 + _bc_rows(g2_ref[...], tm) * _dot(gated, w_out[...])
    if final:
        x2 = x2 * lax.rsqrt(jnp.mean(x2 * x2, axis=-1, keepdims=True) + RMS_EPS) * nout_ref[...]
    y_ref[...] = x2


def _post_ffn(grp, layer, x, mo, w_o, b_o, gain, prm, state, norm_out, tm, final):
    nb, rows, d = x.shape
    dff = prm['w_dw'].shape[1]
    halo = max(SUBLANES, (FFN_CONV_W - 1) * grp.sd)
    row = lambda a: a.reshape(1, -1)
    ins = [x, mo, grp.mod, grp.mod, grp.mod, grp.mod, w_o, row(b_o), gain, prm['w_in'], prm['w_dw'],
           row(prm['b_dw']), prm['w_out'], row(norm_out)]
    in_specs = [grp.row_spec(tm, d), grp.row_spec(tm, d), grp.mod_spec(layer, 2, d),
                grp.mod_spec(layer, 3, d), grp.mod_spec(layer, 4, d), grp.mod_spec(layer, 5, d)]
    in_specs += [_resident(a.shape) for a in ins[6:]]
    if grp.has_state:
        ins.append(state)
        in_specs.append(_resident(state.shape))
    kern = functools.partial(_post_ffn_body, sd=grp.sd, halo=halo, has_state=grp.has_state, final=final)
    return pl.pallas_call(
        kern, grid=(nb, rows // tm), in_specs=in_specs,
        out_specs=[grp.row_spec(tm, d), pl.BlockSpec((None, halo, dff), lambda b, j: (b, 0, 0))],
        out_shape=[jax.ShapeDtypeStruct((nb, rows, d), F32), jax.ShapeDtypeStruct((nb, halo, dff), F32)],
        scratch_shapes=[pltpu.VMEM((halo + tm, dff), F32)],
        compiler_params=_cparams("arbitrary", "arbitrary"), name="post_ffn",
    )(*ins)


def _trunk(grp, x, pos, P, states, paged, tiles):
    nb, rows, d = x.shape
    depth = P['norm_mix'].shape[0]
    out = dict(k=[], v=[], wkv=[], shift=[], conv=[], ffn=[])
    tabs = _rope_tables(pos)
    for i in range(depth):
        kind, j = i % N_MIXERS, i // N_MIXERS
        gain_mix = P['norm_mix'][i].reshape(1, d)
        zeros_d = jnp.zeros((d,), F32)
        if kind == 0:
            lam_init = 0.8 - 0.6 * math.exp(-0.3 * i)
            lvec = [P[n][j].reshape(1, DH_A) for n in ('a_lq1', 'a_lk1', 'a_lq2', 'a_lk2')]
            subln = P['a_subln'][j].reshape(1, DV_A)
            head_major = paged is None
            k, v, q, kh, vh = _pre_a(grp, i, x, gain_mix, P['a_w_qkv'][j], tabs, tiles['pre_a'], head_major)
            if paged is None:
                mo = _attn_prompt(q, kh, vh, *lvec, subln, lam_init, tiles['attn_q'])
            else:
                cache_k, cache_v, page_table = paged
                db = page_table.shape[0]
                t_new = rows // db
                tm_view = lambda a: a.reshape(t_new, db * a.shape[-1])
                mo = _attn_sample(tm_view(q), tm_view(k), tm_view(v), cache_k, cache_v, page_table, j,
                                  *lvec, subln, lam_init).reshape(nb, rows, d)
            out['k'].append(k)
            out['v'].append(v)
            w_o, b_o = P['a_w_o'][j], zeros_d
        elif kind == 1:
            prm = {n[2:]: P[n][j] for n in P if n.startswith('b_')}
            st_shift, st_wkv = (states['shift'][j], states['wkv'][j]) if grp.has_state else (None, None)
            r, wl, k, v, a, g, hl = _pre_b(grp, i, x, gain_mix, prm, st_shift, tiles['pre_b'])
            seqs = (r, wl, k, v, a, g)
            if grp.has_state:
                t_new = rows // grp.sd
                seqs = tuple(s.reshape(t_new, grp.sd * d) for s in seqs)
                mo, wkv = _wkv(seqs, prm, st_wkv, t_new, True)
                mo = mo.reshape(nb, rows, d)
            else:
                mo, wkv = _wkv(seqs, prm, None, tiles['wkv'], False)
            out['shift'].append(hl)
            out['wkv'].append(wkv)
            w_o, b_o = prm['w_o'], zeros_d
        else:
            prm = {n[3:]: P[n][j] for n in P if n.startswith('cv_')}
            st = states['conv'][j] if grp.has_state else None
            mo, cache = _pre_c(grp, i, x, gain_mix, prm, st, tiles['pre_c'])
            out['conv'].append(cache)
            w_o, b_o = prm['w_pw2'], prm['b_pw2']
        prm = {n[2:]: P[n][i] for n in ('f_w_in', 'f_w_dw', 'f_b_dw', 'f_w_out')}
        st = states['ffn'][i] if grp.has_state else None
        x, fcache = _post_ffn(grp, i, x, mo, w_o, b_o, P['norm_ffn'][i].reshape(1, d), prm, st,
                              P['norm_out'], tiles['ffn'], i == depth - 1)
        out['ffn'].append(fcache)
    return x, out


def kernel(x_prompt, x_sample, cache_k, cache_v, state_wkv, state_shift, state_conv, state_ffn, page_table, c_prompt, c_sample, norm_mix, norm_ffn, w_ada, b_ada, a_w_qkv, a_w_o, a_lq1, a_lk1, a_lq2, a_lk2, a_subln, b_mu, b_w_r, b_w_k, b_w_v, b_w_o, b_w0, b_w1, b_w2, b_a0, b_a1, b_a2, b_g1, b_g2, b_k_k, b_k_a, b_r_k, b_ln_g, b_ln_b, cv_w_pw1, cv_b_pw1, cv_w_dw, cv_b_dw, cv_ln_g, cv_ln_b, cv_w_pw2, cv_b_pw2, f_w_in, f_w_dw, f_b_dw, f_w_out, norm_out):
    bsz, seq, d = x_prompt.shape
    db, t_new, _ = x_sample.shape
    depth = norm_mix.shape[0]
    past = page_table.shape[1] * PAGE_SIZE
    bf = lambda a: a.astype(BF16)
    P = dict(norm_mix=norm_mix, norm_ffn=norm_ffn, norm_out=norm_out,
             a_w_qkv=bf(a_w_qkv), a_w_o=bf(a_w_o), a_lq1=a_lq1, a_lk1=a_lk1, a_lq2=a_lq2, a_lk2=a_lk2,
             a_subln=a_subln, b_mu=b_mu, b_w_r=bf(b_w_r), b_w_k=bf(b_w_k), b_w_v=bf(b_w_v),
             b_w_o=bf(b_w_o), b_w0=b_w0, b_w1=bf(b_w1), b_w2=bf(b_w2), b_a0=b_a0, b_a1=bf(b_a1),
             b_a2=bf(b_a2), b_g1=bf(b_g1), b_g2=bf(b_g2), b_k_k=b_k_k, b_k_a=b_k_a, b_r_k=b_r_k,
             b_ln_g=b_ln_g, b_ln_b=b_ln_b, cv_w_pw1=bf(cv_w_pw1), cv_b_pw1=cv_b_pw1, cv_w_dw=cv_w_dw,
             cv_b_dw=cv_b_dw, cv_ln_g=cv_ln_g, cv_ln_b=cv_ln_b, cv_w_pw2=bf(cv_w_pw2),
             cv_b_pw2=cv_b_pw2, f_w_in=bf(f_w_in), f_w_dw=f_w_dw, f_b_dw=f_b_dw, f_w_out=bf(f_w_out))

    pad = (-bsz) % SUBLANES
    c_all = jnp.concatenate([c_prompt, jnp.zeros((pad, d), F32), c_sample], axis=0)
    mod = _adaln(c_all, w_ada, b_ada)
    mod_p = mod[:, :bsz].reshape(depth, bsz, 1, 6 * d)
    mod_s = mod[:, bsz + pad:]

    grp_p = _Group(bsz, seq, 1, mod_p, False)
    tiles_p = dict(pre_a=min(512, seq), attn_q=min(1024, seq), pre_b=min(512, seq), wkv=64,
                   pre_c=min(256, seq), ffn=min(256, seq))
    y_p, o_p = _trunk(grp_p, x_prompt, jnp.arange(seq), P, None, None, tiles_p)

    rows_s = t_new * db
    to_tm = lambda a: jnp.moveaxis(a, 1, 0)
    x_s = to_tm(x_sample).reshape(1, rows_s, d)
    states = dict(
        shift=[state_shift[j] for j in range(state_shift.shape[0])],
        wkv=[state_wkv[j] for j in range(state_wkv.shape[0])],
        conv=[to_tm(state_conv[j]).reshape(-1, d) for j in range(state_conv.shape[0])],
        ffn=[to_tm(state_ffn[i]).reshape(-1, state_ffn.shape[-1]) for i in range(depth)])
    grp_s = _Group(1, rows_s, db, mod_s, True)
    tiles_s = dict(pre_a=rows_s, pre_b=rows_s // 2, pre_c=rows_s, ffn=2 * db)
    pos_s = jnp.repeat(past + jnp.arange(t_new), db)
    n_a, n_pool = cache_k.shape[:2]
    cache_kt = jnp.transpose(cache_k, (0, 1, 3, 4, 2)).reshape(n_a, n_pool, 2 * H_A * DH_A, PAGE_SIZE)
    cache_v2 = cache_v.reshape(n_a, n_pool, PAGE_SIZE * H_A, DV_A)
    y_s, o_s = _trunk(grp_s, x_s, pos_s, P, states, (cache_kt, cache_v2, page_table), tiles_s)

    from_tm = lambda a, w: jnp.moveaxis(a.reshape(w, db, a.shape[-1]), 0, 1)
    dff = f_w_dw.shape[-1]
    n_keep = CONV_W - 1
    outs = (
        y_p,
        from_tm(y_s, t_new),
        jnp.stack(o_p['k']).reshape(-1, bsz, seq, 2 * H_A, DH_A),
        jnp.stack(o_p['v']).reshape(-1, bsz, seq, H_A, DV_A),
        jnp.stack(o_p['wkv']),
        jnp.stack([h[:, -1] for h in o_p['shift']]),
        jnp.stack([c[:, -n_keep:] for c in o_p['conv']]),
        jnp.stack([f[:, -(FFN_CONV_W - 1):] for f in o_p['ffn']]),
        jnp.stack([from_tm(k, t_new) for k in o_s['k']]).reshape(-1, db, t_new, 2 * H_A, DH_A),
        jnp.stack([from_tm(v, t_new) for v in o_s['v']]).reshape(-1, db, t_new, H_A, DV_A),
        jnp.stack(o_s['wkv']),
        jnp.stack([h[0] for h in o_s['shift']]),
        jnp.stack([from_tm(jnp.concatenate([st.reshape(n_keep, db, d), glu.reshape(t_new, db, d)], axis=0)
                           [-n_keep:].reshape(-1, d), n_keep)
                   for st, glu in zip(states['conv'], o_s['conv'])]),
        jnp.stack([from_tm(f, FFN_CONV_W - 1) for f in o_s['ffn']]),
    )
    return outs
```

```python
import functools
import math

import numpy as np
import jax
import jax.numpy as jnp
from jax import lax
from jax.experimental import pallas as pl
from jax.experimental.pallas import tpu as pltpu

F32 = jnp.float32
BF16 = jnp.bfloat16

N_MIXERS = 3
H_A = 8
DH_A = 64
DV_A = 128
ROPE_THETA = 10000.0
PAGE_SIZE = 128
NEG_INF = -1e30
HEAD_B = 64
GN_EPS_B = 64e-5
LN_EPS = 1e-5
RMS_EPS = 1e-6
CONV_W = 31
FFN_CONV_W = 3

SUBLANES = 8
LANES = 128
VMEM_LIMIT_BYTES = 56 * 1024 * 1024

ATTN_RC = 256
ATTN_KC = 256
ATTN_DEPTH = 4
VT_ROWS = DV_A + 16


def _cparams(*sem):
    return pltpu.CompilerParams(dimension_semantics=sem, vmem_limit_bytes=VMEM_LIMIT_BYTES)


def _resident(shape):
    nd = len(shape)
    return pl.BlockSpec(shape, lambda *_: (0,) * nd, pipeline_mode=pl.Buffered(1))


def _dot(a, b):
    return jnp.dot(a, b, preferred_element_type=F32)


def _dot_nt(a, b):
    return lax.dot_general(a, b, (((1,), (1,)), ((), ())), preferred_element_type=F32)


def _dot_tn(a, b):
    return lax.dot_general(a, b, (((0,), (0,)), ((), ())), preferred_element_type=F32)


def _sigmoid(x):
    return 1.0 / (1.0 + jnp.exp(-x))


def _silu(x):
    return x * _sigmoid(x)


def _bc_rows(m, tm):
    nb, c = m.shape
    if nb == 1 or nb == tm:
        return m
    return jnp.broadcast_to(m[None], (tm // nb, nb, c)).reshape(tm, c)


def _modnorm(x, gain, shift, scale):
    tm = x.shape[0]
    y = x * lax.rsqrt(jnp.mean(x * x, axis=-1, keepdims=True) + RMS_EPS) * gain
    return y * (1.0 + _bc_rows(scale, tm)) + _bc_rows(shift, tm)


class _Group:
    def __init__(self, nb, rows, sd, mod, has_state):
        self.nb = nb
        self.rows = rows
        self.sd = sd
        self.mod = mod
        self.has_state = has_state

    def mod_spec(self, layer, k, d):
        if self.mod.ndim == 4:
            return pl.BlockSpec((None, None, 1, d), lambda b, j: (layer, b, 0, k))
        return pl.BlockSpec((None, self.mod.shape[1], d), lambda b, j: (layer, 0, k))

    def row_spec(self, tm, c):
        return pl.BlockSpec((None, tm, c), lambda b, j: (b, j, 0))


def _adaln_body(c_ref, w_ref, b_ref, o_ref):
    sc = _silu(c_ref[...]).astype(BF16)
    o_ref[...] = _dot(sc, w_ref[...].astype(BF16)) + b_ref[...]


def _adaln(c_all, w_ada, b_ada):
    depth, d, n = w_ada.shape
    bp = c_all.shape[0]
    tn = 1536
    return pl.pallas_call(
        _adaln_body,
        grid=(depth, n // tn),
        in_specs=[pl.BlockSpec((bp, d), lambda i, j: (0, 0)),
                  pl.BlockSpec((None, d, tn), lambda i, j: (i, 0, j)),
                  pl.BlockSpec((None, 1, tn), lambda i, j: (i, 0, j))],
        out_specs=pl.BlockSpec((None, bp, tn), lambda i, j: (i, 0, j)),
        out_shape=jax.ShapeDtypeStruct((depth, bp, n), F32),
        compiler_params=_cparams("arbitrary", "arbitrary"),
        name="adaln",
    )(c_all, w_ada, b_ada.reshape(depth, 1, n))


def _pre_a_body(x_ref, sh_ref, sc_ref, gain_ref, w_ref, cos_ref, sina_ref, sinb_ref,
                k_out, v_out, q_out, kh_out, vh_out, *, head_major, q_scale):
    h = _modnorm(x_ref[...], gain_ref[...], sh_ref[...], sc_ref[...]).astype(BF16)
    qkv = _dot(h, w_ref[...])
    cos, sina, sinb = cos_ref[...], sina_ref[...], sinb_ref[...]
    nqk = H_A * LANES
    for s in range(2 * H_A):
        xs = qkv[:, LANES * s:LANES * (s + 1)]
        ro = xs * cos + pltpu.roll(xs, LANES - 32, 1) * sina + pltpu.roll(xs, 32, 1) * sinb
        if s < H_A:
            if head_major:
                q_out[s] = (ro * q_scale).astype(BF16)
            else:
                q_out[:, LANES * s:LANES * (s + 1)] = ro * q_scale
        else:
            hh = s - H_A
            k_out[:, LANES * hh:LANES * (hh + 1)] = ro
            if head_major:
                kh_out[hh] = ro.astype(BF16)
    v = qkv[:, 2 * nqk:]
    v_out[...] = v
    if head_major:
        for hh in range(H_A):
            vh_out[hh, 0:DV_A, :] = v[:, LANES * hh:LANES * (hh + 1)].T.astype(BF16)
            vh_out[hh, DV_A:VT_ROWS, :] = jnp.ones((VT_ROWS - DV_A, v.shape[0]), BF16)


def _rope_tables(pos):
    half = DH_A // 2
    inv = jnp.exp(-math.log(ROPE_THETA) * jnp.arange(half, dtype=F32) * (2.0 / DH_A))
    ang = pos.astype(F32)[:, None] * inv[None, :]
    cos, sin = jnp.cos(ang), jnp.sin(ang)
    zero = jnp.zeros_like(sin)
    cos_t = jnp.concatenate([cos, cos, cos, cos], axis=1)
    sina = jnp.concatenate([-sin, zero, -sin, zero], axis=1)
    sinb = jnp.concatenate([zero, sin, zero, sin], axis=1)
    return cos_t, sina, sinb


def _pre_a(grp, layer, x, gain, w_qkv, tabs, tm, head_major):
    nb, rows, d = x.shape
    n = w_qkv.shape[1]
    dk = H_A * LANES
    q_scale = DH_A ** -0.5 * (math.log2(math.e) if head_major else 1.0)
    kern = functools.partial(_pre_a_body, head_major=head_major, q_scale=q_scale)
    tab_spec = pl.BlockSpec((tm, LANES), lambda b, j: (j, 0))
    in_specs = [grp.row_spec(tm, d), grp.mod_spec(layer, 0, d), grp.mod_spec(layer, 1, d),
                _resident((1, d)), _resident((d, n)), tab_spec, tab_spec, tab_spec]
    out_specs = [grp.row_spec(tm, dk), grp.row_spec(tm, dk)]
    out_shape = [jax.ShapeDtypeStruct((nb, rows, dk), F32), jax.ShapeDtypeStruct((nb, rows, dk), F32)]
    if head_major:
        hspec = pl.BlockSpec((None, H_A, tm, LANES), lambda b, j: (b, 0, j, 0))
        tspec = pl.BlockSpec((None, H_A, VT_ROWS, tm), lambda b, j: (b, 0, 0, j))
        out_specs += [hspec, hspec, tspec]
        out_shape += [jax.ShapeDtypeStruct((nb, H_A, rows, LANES), BF16)] * 2
        out_shape += [jax.ShapeDtypeStruct((nb, H_A, VT_ROWS, rows), BF16)]
    else:
        dummy = pl.BlockSpec((None, SUBLANES, LANES), lambda b, j: (b, 0, 0))
        out_specs += [grp.row_spec(tm, dk), dummy, dummy]
        out_shape += [jax.ShapeDtypeStruct((nb, rows, dk), F32),
                      jax.ShapeDtypeStruct((nb, SUBLANES, LANES), BF16),
                      jax.ShapeDtypeStruct((nb, SUBLANES, LANES), BF16)]
    return pl.pallas_call(
        kern, grid=(nb, rows // tm), in_specs=in_specs, out_specs=out_specs, out_shape=out_shape,
        compiler_params=_cparams("arbitrary", "arbitrary"), name="pre_attn",
    )(x, grp.mod, grp.mod, gain, w_qkv, *tabs)


def _lam(lq1, lk1, lq2, lk2, lam_init):
    return (jnp.exp(jnp.sum(lq1 * lk1, axis=-1, keepdims=True))
            - jnp.exp(jnp.sum(lq2 * lk2, axis=-1, keepdims=True)) + lam_init)


def _diff_finish(acc, l, lam, g, lam_init, n):
    o = acc[0:n] / l[0:n] - lam * (acc[n:2 * n] / l[n:2 * n])
    return o * lax.rsqrt(jnp.mean(o * o, axis=-1, keepdims=True) + RMS_EPS) * g * (1.0 - lam_init)


def _attn_p_body(qi_tab, ki_tab, q_ref, k_ref, vt_ref, lq1, lk1, lq2, lk2, g_ref, o_ref,
                 qs_sc, m_sc, acc_sc, *, tq, rc, kc, depth, lam_init):
    t = pl.program_id(2)
    qi = qi_tab[t]
    ki = ki_tab[t]

    @pl.when(ki == 0)
    def _():
        q = q_ref[...]
        lane = lax.broadcasted_iota(jnp.int32, q.shape, 1)
        zero = jnp.zeros_like(q)
        qs_sc[0:tq, :] = jnp.where(lane < DH_A, q, zero)
        qs_sc[tq:2 * tq, :] = jnp.where(lane >= DH_A, q, zero)
        m_sc[...] = jnp.full(m_sc.shape, NEG_INF, F32)
        acc_sc[...] = jnp.zeros(acc_sc.shape, F32)

    def scores(blk):
        r0, k0, masked = blk
        st = _dot_nt(k_ref[k0:k0 + kc, :], qs_sc[r0:r0 + rc, :])
        if masked:
            kpos = lax.broadcasted_iota(jnp.int32, st.shape, 0) + k0
            qpos = lax.broadcasted_iota(jnp.int32, st.shape, 1) + (r0 % tq)
            st = jnp.where(kpos <= qpos, st, NEG_INF)
        return st

    def softmax_pv(blk, st):
        r0, k0, _ = blk
        cols = slice(r0, r0 + rc)
        m_prev = m_sc[:, cols]
        m_new = jnp.maximum(m_prev, jnp.max(st, axis=0, keepdims=True))
        alpha = jnp.exp2(m_prev - m_new)
        pt = jnp.exp2(st - m_new).astype(BF16)
        m_sc[:, cols] = m_new
        acc_sc[:, cols] = alpha * acc_sc[:, cols] + _dot(vt_ref[:, k0:k0 + kc], pt)

    def run(blocks):
        pending = []
        for blk in blocks:
            pending.append((blk, scores(blk)))
            if len(pending) > depth:
                softmax_pv(*pending.pop(0))
        for item in pending:
            softmax_pv(*item)

    @pl.when(ki < qi)
    def _():
        run([(r0, k0, False) for k0 in range(0, tq, kc) for r0 in range(0, 2 * tq, rc)])

    @pl.when(ki == qi)
    def _():
        blocks = []
        for k0 in range(0, tq, kc):
            for r0 in range(0, 2 * tq, rc):
                q_lo = r0 % tq
                if k0 <= q_lo + rc - 1:
                    blocks.append((r0, k0, k0 + kc - 1 > q_lo))
        run(blocks)
        lam = _lam(lq1[...], lk1[...], lq2[...], lk2[...], lam_init)
        l_tot = acc_sc[DV_A:DV_A + 1, :]
        acc = acc_sc[0:DV_A, :]
        ot = acc[:, 0:tq] / l_tot[:, 0:tq] - lam * (acc[:, tq:2 * tq] / l_tot[:, tq:2 * tq])
        ot = (ot * lax.rsqrt(jnp.mean(ot * ot, axis=0, keepdims=True) + RMS_EPS)
              * g_ref[...] * (1.0 - lam_init))
        o_ref[...] = ot.T.astype(BF16)


def _attn_prompt(qh, kh, vth, lq1, lk1, lq2, lk2, g, lam_init, tq):
    nb, nh, s, _ = qh.shape
    nq = s // tq
    rc = min(ATTN_RC, tq)
    kc = min(ATTN_KC, tq)
    pairs = [(qi, ki) for qi in range(nq) for ki in range(qi + 1)]
    qi_tab = jnp.asarray(np.array([p[0] for p in pairs], np.int32))
    ki_tab = jnp.asarray(np.array([p[1] for p in pairs], np.int32))
    small = pl.BlockSpec((1, DH_A), lambda b, h, t, qt, kt: (0, 0))
    grid_spec = pltpu.PrefetchScalarGridSpec(
        num_scalar_prefetch=2,
        grid=(nb, nh, len(pairs)),
        in_specs=[pl.BlockSpec((None, None, tq, LANES), lambda b, h, t, qt, kt: (b, h, qt[t], 0)),
                  pl.BlockSpec((None, None, tq, LANES), lambda b, h, t, qt, kt: (b, h, kt[t], 0)),
                  pl.BlockSpec((None, None, VT_ROWS, tq), lambda b, h, t, qt, kt: (b, h, 0, kt[t])),
                  small, small, small, small,
                  pl.BlockSpec((DV_A, 1), lambda b, h, t, qt, kt: (0, 0))],
        out_specs=pl.BlockSpec((None, tq, LANES), lambda b, h, t, qt, kt: (b, qt[t], h)),
        scratch_shapes=[pltpu.VMEM((2 * tq, LANES), BF16), pltpu.VMEM((1, 2 * tq), F32),
                        pltpu.VMEM((VT_ROWS, 2 * tq), F32)])
    return pl.pallas_call(
        functools.partial(_attn_p_body, tq=tq, rc=rc, kc=kc, depth=ATTN_DEPTH, lam_init=lam_init),
        grid_spec=grid_spec,
        out_shape=jax.ShapeDtypeStruct((nb, s, nh * LANES), BF16),
        compiler_params=_cparams("arbitrary", "arbitrary", "arbitrary"), name="attn_prompt",
    )(qi_tab, ki_tab, qh, kh, vth, lq1, lk1, lq2, lk2, g)


def _attn_s_body(pt, q_ref, kn_ref, vn_ref, *rest, n_pages, lam_init):
    k_refs = rest[:n_pages]
    v_refs = rest[n_pages:2 * n_pages]
    lq1, lk1, lq2, lk2, g_ref, o_ref = rest[2 * n_pages:]
    t_new, dk = q_ref.shape
    n_sub = 2 * H_A
    n_rows = n_sub * t_new
    page = k_refs[0].shape[1]

    q_tiled = jnp.concatenate([q_ref[...]] * n_sub, axis=0)
    ri = lax.broadcasted_iota(jnp.int32, (n_rows, dk), 0)
    ci = lax.broadcasted_iota(jnp.int32, (n_rows, dk), 1)
    same_sub = (lax.shift_right_logical(ri, int(math.log2(t_new)))
                == lax.shift_right_logical(ci, int(math.log2(DH_A))))
    qbd = jnp.where(same_sub, q_tiled, jnp.zeros_like(q_tiled)).astype(BF16)

    s_past = jnp.concatenate([_dot(qbd, k_refs[p][...].astype(BF16)) for p in range(n_pages)], axis=1)
    s_new = _dot_nt(qbd, kn_ref[...].astype(BF16))
    row = lax.broadcasted_iota(jnp.int32, s_new.shape, 0)
    col = lax.broadcasted_iota(jnp.int32, s_new.shape, 1)
    s_new = jnp.where(col <= jnp.bitwise_and(row, t_new - 1), s_new, NEG_INF)
    m = jnp.maximum(jnp.max(s_past, axis=1, keepdims=True), jnp.max(s_new, axis=1, keepdims=True))
    p_past = jnp.exp(s_past - m)
    p_new = jnp.exp(s_new - m)
    l = jnp.sum(p_past, axis=1, keepdims=True) + jnp.sum(p_new, axis=1, keepdims=True)
    p_past = p_past.astype(BF16)

    lam = _lam(lq1[...], lk1[...], lq2[...], lk2[...], lam_init)
    vn = vn_ref[...]
    for hh in range(H_A):
        rows = slice(2 * t_new * hh, 2 * t_new * (hh + 1))
        v_h = jnp.concatenate([v_refs[p][pl.ds(hh, page, stride=H_A), :] for p in range(n_pages)],
                              axis=0).astype(BF16)
        acc = _dot(p_past[rows, :], v_h) + _dot(p_new[rows, :], vn[:, LANES * hh:LANES * (hh + 1)])
        o = _diff_finish(acc, l[rows, :], lam, g_ref[...], lam_init, t_new)
        o_ref[:, LANES * hh:LANES * (hh + 1)] = o.astype(BF16)


def _attn_sample(q, k_new, v_new, cache_kt, cache_v2, page_table, j, lq1, lk1, lq2, lk2, g, lam_init):
    t_new, width = q.shape
    db, n_pages = page_table.shape
    dk = width // db
    small = pl.BlockSpec((1, DH_A), lambda b, pt: (0, 0))
    seq = pl.BlockSpec((t_new, dk), lambda b, pt: (0, b))

    def page_spec(arr, p):
        return pl.BlockSpec((None, None) + arr.shape[2:], lambda b, pt: (j, pt[b, p], 0, 0))

    grid_spec = pltpu.PrefetchScalarGridSpec(
        num_scalar_prefetch=1,
        grid=(db,),
        in_specs=([seq, seq, seq] + [page_spec(cache_kt, p) for p in range(n_pages)]
                  + [page_spec(cache_v2, p) for p in range(n_pages)]
                  + [small, small, small, small, pl.BlockSpec((1, DV_A), lambda b, pt: (0, 0))]),
        out_specs=seq)
    return pl.pallas_call(
        functools.partial(_attn_s_body, n_pages=n_pages, lam_init=lam_init),
        grid_spec=grid_spec,
        out_shape=jax.ShapeDtypeStruct((t_new, width), BF16),
        compiler_params=_cparams("arbitrary"), name="attn_sample",
    )(page_table, q, k_new, v_new, *([cache_kt] * n_pages), *([cache_v2] * n_pages),
      lq1, lk1, lq2, lk2, g)


def _pre_b_body(*refs, sd, halo, has_state):
    (x_ref, sh_ref, sc_ref, gain_ref, mu_ref, wr, wk, wv, w0, w1, w2, a0, a1, a2, g1, g2) = refs[:16]
    rest = refs[16:]
    if has_state:
        st_ref, rest = rest[0], rest[1:]
    r_out, wl_out, k_out, v_out, a_out, g_out, hl_out, ext = rest
    j = pl.program_id(1)
    tm = x_ref.shape[0]

    @pl.when(j == 0)
    def _():
        if has_state:
            ext[0:halo, :] = st_ref[...]
        else:
            ext[0:halo, :] = jnp.zeros((halo, ext.shape[1]), F32)

    h = _modnorm(x_ref[...], gain_ref[...], sh_ref[...], sc_ref[...])
    ext[halo:halo + tm, :] = h
    xx = ext[pl.ds(halo - sd, tm), :] - h
    ext[0:halo, :] = ext[tm:tm + halo, :]
    hl_out[...] = h[tm - halo:tm]

    mu = mu_ref[...]
    mix = lambda n: (h + xx * mu[n:n + 1]).astype(BF16)
    r_out[...] = _dot(mix(0), wr[...])
    wraw = w0[...] + _dot(jnp.tanh(_dot(mix(1), w1[...])).astype(BF16), w2[...])
    k_out[...] = _dot(mix(2), wk[...])
    v_out[...] = _dot(mix(3), wv[...])
    a_out[...] = _sigmoid(a0[...] + _dot(_dot(mix(4), a1[...]).astype(BF16), a2[...]))
    g_out[...] = _dot(_sigmoid(_dot(mix(5), g1[...])).astype(BF16), g2[...])
    z = -wraw
    softplus = jnp.maximum(z, 0.0) + jnp.log(1.0 + jnp.exp(-jnp.abs(z)))
    wl_out[...] = -jnp.exp(-softplus - 0.5)


def _pre_b(grp, layer, x, gain, prm, state, tm):
    nb, rows, d = x.shape
    halo = max(SUBLANES, grp.sd)
    kern = functools.partial(_pre_b_body, sd=grp.sd, halo=halo, has_state=grp.has_state)
    row = lambda a: a.reshape(1, -1)
    ins = [x, grp.mod, grp.mod, gain, prm['mu'], prm['w_r'], prm['w_k'], prm['w_v'], row(prm['w0']),
           prm['w1'], prm['w2'], row(prm['a0']), prm['a1'], prm['a2'], prm['g1'], prm['g2']]
    in_specs = [grp.row_spec(tm, d), grp.mod_spec(layer, 0, d), grp.mod_spec(layer, 1, d)]
    in_specs += [_resident(a.shape) for a in ins[3:]]
    if grp.has_state:
        ins.append(state)
        in_specs.append(_resident(state.shape))
    big = jax.ShapeDtypeStruct((nb, rows, d), F32)
    out_specs = [grp.row_spec(tm, d)] * 6 + [pl.BlockSpec((None, halo, d), lambda b, j: (b, 0, 0))]
    out_shape = [big] * 6 + [jax.ShapeDtypeStruct((nb, halo, d), F32)]
    return pl.pallas_call(
        kern, grid=(nb, rows // tm), in_specs=in_specs, out_specs=out_specs, out_shape=out_shape,
        scratch_shapes=[pltpu.VMEM((halo + tm, d), F32)],
        compiler_params=_cparams("arbitrary", "arbitrary"), name="pre_rwkv",
    )(*ins)


def _wkv_body(*refs, L, has_state, mm_dtype):
    (r_ref, wl_ref, k_ref, v_ref, a_ref, g_ref, kk_ref, ka_ref, rk_ref, lng_ref, lnb_ref) = refs[:11]
    rest = refs[11:]
    if has_state:
        s0_ref, rest = rest[0], rest[1:]
    mo_ref, sout_ref, g_sc = rest
    c = pl.program_id(1)
    n_pairs = g_sc.shape[0]
    hb = HEAD_B
    mm = lambda x: x.astype(mm_dtype)

    @pl.when(c == 0)
    def _():
        if has_state:
            zero = jnp.zeros((hb, hb), F32)
            for pr in range(n_pairs):
                top = jnp.concatenate([s0_ref[2 * pr], zero], axis=1)
                bot = jnp.concatenate([zero, s0_ref[2 * pr + 1]], axis=1)
                g_sc[pr] = jnp.concatenate([top, bot], axis=0)
        else:
            g_sc[...] = jnp.zeros(g_sc.shape, F32)

    wl = wl_ref[...]
    ti = lax.broadcasted_iota(jnp.int32, (L, L), 0)
    tj = lax.broadcasted_iota(jnp.int32, (L, L), 1)
    tri = jnp.where(tj <= ti, 1.0, 0.0).astype(F32)
    cum = jnp.dot(tri, wl, preferred_element_type=F32, precision=lax.Precision.HIGHEST)
    gam = jnp.exp(cum)
    gam_prev = jnp.exp(cum - wl)
    inv_gam = jnp.exp(-cum)
    gam_last = gam[L - 1:L]

    lane = lax.broadcasted_iota(jnp.int32, (L, LANES), 1)
    in_h0 = lane < hb
    bi = lax.broadcasted_iota(jnp.int32, (LANES, LANES), 0)
    bj = lax.broadcasted_iota(jnp.int32, (LANES, LANES), 1)
    same_head = (bi < hb) == (bj < hb)
    ones_bd = jnp.where(same_head, 1.0, 0.0).astype(BF16)

    def headsum(x):
        return _dot(x.astype(BF16), ones_bd)

    def stack(x):
        zero = jnp.zeros_like(x)
        return jnp.concatenate([jnp.where(in_h0, x, zero), jnp.where(in_h0, zero, x)], axis=0)

    wr = lax.broadcasted_iota(jnp.int32, (L, 2 * L), 0)
    wc = lax.broadcasted_iota(jnp.int32, (L, 2 * L), 1)
    wt = jnp.where(wc >= L, wc - L, wc)
    strict = wt < wr
    incl = wt <= wr
    w_h0 = wc < L

    def bd(w):
        zero = jnp.zeros_like(w)
        return jnp.concatenate([jnp.where(w_h0, w, zero), jnp.where(w_h0, zero, w)], axis=0)

    r_all, k_all, v_all, a_all, g_all = r_ref[...], k_ref[...], v_ref[...], a_ref[...], g_ref[...]
    prs = range(n_pairs)
    sls = [slice(LANES * pr, LANES * (pr + 1)) for pr in prs]
    a = [a_all[:, sl] for sl in sls]
    kraw = [k_all[:, sl] for sl in sls]
    v = [v_all[:, sl] for sl in sls]
    r = [r_all[:, sl] for sl in sls]
    kk = [kraw[i] * kk_ref[:, sls[i]] for i in prs]
    kmod = [kraw[i] * (1.0 + (a[i] - 1.0) * ka_ref[:, sls[i]]) for i in prs]
    ssq = [headsum(kk[i] * kk[i]) for i in prs]
    bonus_w = [headsum(r[i] * kmod[i] * rk_ref[:, sls[i]]) for i in prs]
    kk = [kk[i] * lax.rsqrt(jnp.maximum(ssq[i], 1e-24)) for i in prs]
    P = [-kk[i] * gam_prev[:, sls[i]] for i in prs]
    Q = [kk[i] * a[i] * inv_gam[:, sls[i]] for i in prs]
    K = [kmod[i] * inv_gam[:, sls[i]] for i in prs]
    R = [r[i] * gam[:, sls[i]] for i in prs]

    z = [_dot_nt(mm(jnp.concatenate([P[i], R[i]], axis=0)),
                 mm(jnp.concatenate([stack(Q[i]), stack(K[i])], axis=0))) for i in prs]
    zero_w = jnp.zeros((L, 2 * L), F32)
    a_pq = [jnp.where(strict, z[i][0:L, 0:2 * L], zero_w) for i in prs]
    a_pk = [jnp.where(strict, z[i][0:L, 2 * L:4 * L], zero_w) for i in prs]
    a_rq = [jnp.where(incl, z[i][L:2 * L, 0:2 * L], zero_w) for i in prs]
    a_rk = [jnp.where(incl, z[i][L:2 * L, 2 * L:4 * L], zero_w) for i in prs]

    nmat = list(a_pq)
    pw = list(a_pq)
    for _ in range(int(math.log2(L)) - 1):
        pw = [_dot(mm(pw[i]), mm(bd(pw[i]))) for i in prs]
        nmat = [nmat[i] + pw[i] + _dot(mm(nmat[i]), mm(bd(pw[i]))) for i in prs]

    gmat = [g_sc[i] for i in prs]
    gb = [mm(gmat[i]) for i in prs]
    sv = [mm(stack(v[i])) for i in prs]
    rhs = [_dot_nt(mm(P[i]), gb[i]) + _dot(mm(a_pk[i]), sv[i]) for i in prs]
    u = [rhs[i] + _dot(mm(nmat[i]), mm(stack(rhs[i]))) for i in prs]
    y = [_dot_nt(mm(R[i]), gb[i]) + _dot(mm(a_rq[i]), mm(stack(u[i]))) + _dot(mm(a_rk[i]), sv[i])
         for i in prs]
    dg = [_dot_tn(mm(jnp.concatenate([u[i], v[i]], axis=0)), mm(jnp.concatenate([Q[i], K[i]], axis=0)))
          for i in prs]
    for i in prs:
        g_sc[i] = (gmat[i] + jnp.where(same_head, dg[i], jnp.zeros_like(dg[i]))) * gam_last[:, sls[i]]

    mean = [headsum(y[i]) * (1.0 / hb) for i in prs]
    dlt = [y[i] - mean[i] for i in prs]
    var = [headsum(dlt[i] * dlt[i]) * (1.0 / hb) for i in prs]
    for i in prs:
        yn = dlt[i] * lax.rsqrt(var[i] + GN_EPS_B) * lng_ref[:, sls[i]] + lnb_ref[:, sls[i]]
        mo_ref[:, sls[i]] = ((yn + bonus_w[i] * v[i]) * g_all[:, sls[i]]).astype(BF16)

    @pl.when(c == pl.num_programs(1) - 1)
    def _():
        for pr in range(n_pairs):
            gfin = g_sc[pr]
            sout_ref[2 * pr] = gfin[0:hb, 0:hb]
            sout_ref[2 * pr + 1] = gfin[hb:2 * hb, hb:2 * hb]


def _wkv(seqs, prm, state, L, time_major):
    row = lambda a: a.reshape(1, -1)
    d = prm['k_k'].shape[-1]
    n_heads = d // HEAD_B
    if time_major:
        t_len, width = seqs[0].shape
        nb = width // d
        seq_spec = pl.BlockSpec((L, d), lambda b, c: (c, b))
        mo_shape = jax.ShapeDtypeStruct((t_len, width), BF16)
    else:
        nb, t_len, _ = seqs[0].shape
        seq_spec = pl.BlockSpec((None, L, d), lambda b, c: (b, c, 0))
        mo_shape = jax.ShapeDtypeStruct((nb, t_len, d), BF16)
    has_state = state is not None
    st_spec = pl.BlockSpec((None, n_heads, HEAD_B, HEAD_B), lambda b, c: (b, 0, 0, 0))
    ins = list(seqs) + [row(prm['k_k']), row(prm['k_a']), row(prm['r_k']), row(prm['ln_g']), row(prm['ln_b'])]
    in_specs = [seq_spec] * 6 + [_resident((1, d))] * 5
    if has_state:
        ins.append(state)
        in_specs.append(st_spec)
    kern = functools.partial(_wkv_body, L=L, has_state=has_state, mm_dtype=BF16 if L >= 16 else F32)
    return pl.pallas_call(
        kern, grid=(nb, t_len // L), in_specs=in_specs,
        out_specs=[seq_spec, st_spec],
        out_shape=[mo_shape, jax.ShapeDtypeStruct((nb, n_heads, HEAD_B, HEAD_B), F32)],
        scratch_shapes=[pltpu.VMEM((n_heads // 2, LANES, LANES), F32)],
        compiler_params=_cparams("arbitrary", "arbitrary"), name="wkv",
    )(*ins)


def _layernorm_silu(y, g, b):
    mu = jnp.mean(y, axis=-1, keepdims=True)
    dlt = y - mu
    var = jnp.mean(dlt * dlt, axis=-1, keepdims=True)
    return _silu(dlt * lax.rsqrt(var + LN_EPS) * g + b)


def _pre_c_prompt_body(x_ref, sh_ref, sc_ref, gain_ref, w1, b1, wdw, bdw, lng, lnb,
                       mo_ref, cache_ref, ext, shift_sc, *, halo):
    j = pl.program_id(1)
    tm, d = x_ref.shape

    @pl.when(j == 0)
    def _():
        ext[0:halo, :] = jnp.zeros((halo, d), F32)

    h = _modnorm(x_ref[...], gain_ref[...], sh_ref[...], sc_ref[...]).astype(BF16)
    u = _dot(h, w1[...]) + b1[...]
    glu = u[:, :d] * _sigmoid(u[:, d:])
    ext[halo:halo + tm, :] = glu
    span = halo - SUBLANES
    cols_out = []
    for lt in range(d // LANES):
        cols = slice(LANES * lt, LANES * (lt + 1))
        acc = None
        for b in range(SUBLANES):
            shift_sc[b] = ext[pl.ds(SUBLANES - b, tm + span), cols]
            for a in range(halo // SUBLANES):
                dd = SUBLANES * a + b
                if dd >= CONV_W:
                    continue
                off = span - SUBLANES * a
                term = shift_sc[b, off:off + tm, :] * wdw[CONV_W - 1 - dd:CONV_W - dd, cols]
                acc = term if acc is None else acc + term
        cols_out.append(acc)
    y = jnp.concatenate(cols_out, axis=1) + bdw[...]
    tail = ext[tm:tm + halo, :]
    ext[0:halo, :] = tail
    cache_ref[...] = tail
    mo_ref[...] = _layernorm_silu(y, lng[...], lnb[...]).astype(BF16)


def _pre_c_sample_body(x_ref, sh_ref, sc_ref, gain_ref, w1, b1, wdw, bdw, lng, lnb, st_ref,
                       mo_ref, glu_ref, *, nb):
    tm, d = x_ref.shape
    n_t = tm // nb
    n_st = CONV_W - 1
    h = _modnorm(x_ref[...], gain_ref[...], sh_ref[...], sc_ref[...]).astype(BF16)
    u = _dot(h, w1[...]) + b1[...]
    glu = u[:, :d] * _sigmoid(u[:, d:])
    glu_ref[...] = glu
    for t in range(n_t):
        acc = None
        for w in range(CONV_W):
            pos = t + w
            if pos < n_st:
                slab = st_ref[nb * pos:nb * (pos + 1), :]
            else:
                slab = glu[nb * (pos - n_st):nb * (pos - n_st + 1), :]
            term = slab * wdw[w:w + 1, :]
            acc = term if acc is None else acc + term
        mo_ref[nb * t:nb * (t + 1), :] = _layernorm_silu(acc + bdw[...], lng[...], lnb[...]).astype(BF16)


def _pre_c(grp, layer, x, gain, prm, state, tm):
    nb, rows, d = x.shape
    row = lambda a: a.reshape(1, -1)
    ins = [x, grp.mod, grp.mod, gain, prm['w_pw1'], row(prm['b_pw1']), prm['w_dw'], row(prm['b_dw']),
           row(prm['ln_g']), row(prm['ln_b'])]
    in_specs = [grp.row_spec(tm, d), grp.mod_spec(layer, 0, d), grp.mod_spec(layer, 1, d)]
    in_specs += [_resident(a.shape) for a in ins[3:]]
    if grp.has_state:
        db = grp.sd
        ins.append(state)
        in_specs.append(_resident(state.shape))
        return pl.pallas_call(
            functools.partial(_pre_c_sample_body, nb=db), grid=(nb, 1), in_specs=in_specs,
            out_specs=[grp.row_spec(rows, d), grp.row_spec(rows, d)],
            out_shape=[jax.ShapeDtypeStruct((nb, rows, d), BF16), jax.ShapeDtypeStruct((nb, rows, d), F32)],
            compiler_params=_cparams("arbitrary", "arbitrary"), name="pre_conv_sample",
        )(*ins)
    halo = 32
    return pl.pallas_call(
        functools.partial(_pre_c_prompt_body, halo=halo), grid=(nb, rows // tm), in_specs=in_specs,
        out_specs=[grp.row_spec(tm, d), pl.BlockSpec((None, halo, d), lambda b, j: (b, 0, 0))],
        out_shape=[jax.ShapeDtypeStruct((nb, rows, d), BF16), jax.ShapeDtypeStruct((nb, halo, d), F32)],
        scratch_shapes=[pltpu.VMEM((halo + tm, d), F32),
                        pltpu.VMEM((SUBLANES, tm + halo - SUBLANES, LANES), F32)],
        compiler_params=_cparams("arbitrary", "arbitrary"), name="pre_conv_prompt",
    )(*ins)


def _post_ffn_body(*refs, sd, halo, has_state, final):
    (x_ref, mo_ref, g1_ref, sh_ref, sc_ref, g2_ref, wo, bo, gain_ref, w_in, wdw, bdw, w_out,
     nout_ref) = refs[:14]
    rest = refs[14:]
    if has_state:
        st_ref, rest = rest[0], rest[1:]
    y_ref, cache_ref, ext = rest
    j = pl.program_id(1)
    tm = x_ref.shape[0]
    dff = ext.shape[1]

    @pl.when(j == 0)
    def _():
        if has_state:
            ext[0:halo, :] = st_ref[...]
        else:
            ext[0:halo, :] = jnp.zeros((halo, dff), F32)

    x1 = x_ref[...] + _bc_rows(g1_ref[...], tm) * (_dot(mo_ref[...], wo[...]) + bo[...])
    h = _modnorm(x1, gain_ref[...], sh_ref[...], sc_ref[...]).astype(BF16)
    u = _dot(h, w_in[...])
    act = u[:, :dff]
    val = u[:, dff:]
    ext[halo:halo + tm, :] = act
    conv = (act * wdw[2:3, :] + ext[pl.ds(halo - sd, tm), :] * wdw[1:2, :]
            + ext[pl.ds(halo - 2 * sd, tm), :] * wdw[0:1, :] + bdw[...])
    tail = ext[tm:tm + halo, :]
    ext[0:halo, :] = tail
    cache_ref[...] = tail
    gated = (_silu(conv) * val).astype(BF16)
    x2 = x1 + _bc_rows(g2_ref[...], tm) * _dot(gated, w_out[...])
    if final:
        x2 = x2 * lax.rsqrt(jnp.mean(x2 * x2, axis=-1, keepdims=True) + RMS_EPS) * nout_ref[...]
    y_ref[...] = x2


def _post_ffn(grp, layer, x, mo, w_o, b_o, gain, prm, state, norm_out, tm, final):
    nb, rows, d = x.shape
    dff = prm['w_dw'].shape[1]
    halo = max(SUBLANES, (FFN_CONV_W - 1) * grp.sd)
    row = lambda a: a.reshape(1, -1)
    ins = [x, mo, grp.mod, grp.mod, grp.mod, grp.mod, w_o, row(b_o), gain, prm['w_in'], prm['w_dw'],
           row(prm['b_dw']), prm['w_out'], row(norm_out)]
    in_specs = [grp.row_spec(tm, d), grp.row_spec(tm, d), grp.mod_spec(layer, 2, d),
                grp.mod_spec(layer, 3, d), grp.mod_spec(layer, 4, d), grp.mod_spec(layer, 5, d)]
    in_specs += [_resident(a.shape) for a in ins[6:]]
    if grp.has_state:
        ins.append(state)
        in_specs.append(_resident(state.shape))
    kern = functools.partial(_post_ffn_body, sd=grp.sd, halo=halo, has_state=grp.has_state, final=final)
    return pl.pallas_call(
        kern, grid=(nb, rows // tm), in_specs=in_specs,
        out_specs=[grp.row_spec(tm, d), pl.BlockSpec((None, halo, dff), lambda b, j: (b, 0, 0))],
        out_shape=[jax.ShapeDtypeStruct((nb, rows, d), F32), jax.ShapeDtypeStruct((nb, halo, dff), F32)],
        scratch_shapes=[pltpu.VMEM((halo + tm, dff), F32)],
        compiler_params=_cparams("arbitrary", "arbitrary"), name="post_ffn",
    )(*ins)


def _trunk(grp, x, pos, P, states, paged, tiles):
    nb, rows, d = x.shape
    depth = P['norm_mix'].shape[0]
    out = dict(k=[], v=[], wkv=[], shift=[], conv=[], ffn=[])
    tabs = _rope_tables(pos)
    for i in range(depth):
        kind, j = i % N_MIXERS, i // N_MIXERS
        gain_mix = P['norm_mix'][i].reshape(1, d)
        zeros_d = jnp.zeros((d,), F32)
        if kind == 0:
            lam_init = 0.8 - 0.6 * math.exp(-0.3 * i)
            lvec = [P[n][j].reshape(1, DH_A) for n in ('a_lq1', 'a_lk1', 'a_lq2', 'a_lk2')]
            subln = P['a_subln'][j].reshape(1, DV_A)
            head_major = paged is None
            k, v, q, kh, vh = _pre_a(grp, i, x, gain_mix, P['a_w_qkv'][j], tabs, tiles['pre_a'], head_major)
            if paged is None:
                mo = _attn_prompt(q, kh, vh, *lvec, subln.reshape(DV_A, 1), lam_init, tiles['attn_q'])
            else:
                cache_k, cache_v, page_table = paged
                db = page_table.shape[0]
                t_new = rows // db
                tm_view = lambda a: a.reshape(t_new, db * a.shape[-1])
                mo = _attn_sample(tm_view(q), tm_view(k), tm_view(v), cache_k, cache_v, page_table, j,
                                  *lvec, subln, lam_init).reshape(nb, rows, d)
            out['k'].append(k)
            out['v'].append(v)
            w_o, b_o = P['a_w_o'][j], zeros_d
        elif kind == 1:
            prm = {n[2:]: P[n][j] for n in P if n.startswith('b_')}
            st_shift, st_wkv = (states['shift'][j], states['wkv'][j]) if grp.has_state else (None, None)
            r, wl, k, v, a, g, hl = _pre_b(grp, i, x, gain_mix, prm, st_shift, tiles['pre_b'])
            seqs = (r, wl, k, v, a, g)
            if grp.has_state:
                t_new = rows // grp.sd
                seqs = tuple(s.reshape(t_new, grp.sd * d) for s in seqs)
                mo, wkv = _wkv(seqs, prm, st_wkv, t_new, True)
                mo = mo.reshape(nb, rows, d)
            else:
                mo, wkv = _wkv(seqs, prm, None, tiles['wkv'], False)
            out['shift'].append(hl)
            out['wkv'].append(wkv)
            w_o, b_o = prm['w_o'], zeros_d
        else:
            prm = {n[3:]: P[n][j] for n in P if n.startswith('cv_')}
            st = states['conv'][j] if grp.has_state else None
            mo, cache = _pre_c(grp, i, x, gain_mix, prm, st, tiles['pre_c'])
            out['conv'].append(cache)
            w_o, b_o = prm['w_pw2'], prm['b_pw2']
        prm = {n[2:]: P[n][i] for n in ('f_w_in', 'f_w_dw', 'f_b_dw', 'f_w_out')}
        st = states['ffn'][i] if grp.has_state else None
        x, fcache = _post_ffn(grp, i, x, mo, w_o, b_o, P['norm_ffn'][i].reshape(1, d), prm, st,
                              P['norm_out'], tiles['ffn'], i == depth - 1)
        out['ffn'].append(fcache)
    return x, out


def kernel(x_prompt, x_sample, cache_k, cache_v, state_wkv, state_shift, state_conv, state_ffn, page_table, c_prompt, c_sample, norm_mix, norm_ffn, w_ada, b_ada, a_w_qkv, a_w_o, a_lq1, a_lk1, a_lq2, a_lk2, a_subln, b_mu, b_w_r, b_w_k, b_w_v, b_w_o, b_w0, b_w1, b_w2, b_a0, b_a1, b_a2, b_g1, b_g2, b_k_k, b_k_a, b_r_k, b_ln_g, b_ln_b, cv_w_pw1, cv_b_pw1, cv_w_dw, cv_b_dw, cv_ln_g, cv_ln_b, cv_w_pw2, cv_b_pw2, f_w_in, f_w_dw, f_b_dw, f_w_out, norm_out):
    bsz, seq, d = x_prompt.shape
    db, t_new, _ = x_sample.shape
    depth = norm_mix.shape[0]
    past = page_table.shape[1] * PAGE_SIZE
    bf = lambda a: a.astype(BF16)
    P = dict(norm_mix=norm_mix, norm_ffn=norm_ffn, norm_out=norm_out,
             a_w_qkv=bf(a_w_qkv), a_w_o=bf(a_w_o), a_lq1=a_lq1, a_lk1=a_lk1, a_lq2=a_lq2, a_lk2=a_lk2,
             a_subln=a_subln, b_mu=b_mu, b_w_r=bf(b_w_r), b_w_k=bf(b_w_k), b_w_v=bf(b_w_v),
             b_w_o=bf(b_w_o), b_w0=b_w0, b_w1=bf(b_w1), b_w2=bf(b_w2), b_a0=b_a0, b_a1=bf(b_a1),
             b_a2=bf(b_a2), b_g1=bf(b_g1), b_g2=bf(b_g2), b_k_k=b_k_k, b_k_a=b_k_a, b_r_k=b_r_k,
             b_ln_g=b_ln_g, b_ln_b=b_ln_b, cv_w_pw1=bf(cv_w_pw1), cv_b_pw1=cv_b_pw1, cv_w_dw=cv_w_dw,
             cv_b_dw=cv_b_dw, cv_ln_g=cv_ln_g, cv_ln_b=cv_ln_b, cv_w_pw2=bf(cv_w_pw2),
             cv_b_pw2=cv_b_pw2, f_w_in=bf(f_w_in), f_w_dw=f_w_dw, f_b_dw=f_b_dw, f_w_out=bf(f_w_out))

    pad = (-bsz) % SUBLANES
    c_all = jnp.concatenate([c_prompt, jnp.zeros((pad, d), F32), c_sample], axis=0)
    mod = _adaln(c_all, w_ada, b_ada)
    mod_p = mod[:, :bsz].reshape(depth, bsz, 1, 6 * d)
    mod_s = mod[:, bsz + pad:]

    grp_p = _Group(bsz, seq, 1, mod_p, False)
    tiles_p = dict(pre_a=min(512, seq), attn_q=min(1024, seq), pre_b=min(512, seq), wkv=64,
                   pre_c=min(256, seq), ffn=min(256, seq))
    y_p, o_p = _trunk(grp_p, x_prompt, jnp.arange(seq), P, None, None, tiles_p)

    rows_s = t_new * db
    to_tm = lambda a: jnp.moveaxis(a, 1, 0)
    x_s = to_tm(x_sample).reshape(1, rows_s, d)
    states = dict(
        shift=[state_shift[j] for j in range(state_shift.shape[0])],
        wkv=[state_wkv[j] for j in range(state_wkv.shape[0])],
        conv=[to_tm(state_conv[j]).reshape(-1, d) for j in range(state_conv.shape[0])],
        ffn=[to_tm(state_ffn[i]).reshape(-1, state_ffn.shape[-1]) for i in range(depth)])
    grp_s = _Group(1, rows_s, db, mod_s, True)
    tiles_s = dict(pre_a=rows_s, pre_b=rows_s // 2, pre_c=rows_s, ffn=2 * db)
    pos_s = jnp.repeat(past + jnp.arange(t_new), db)
    n_a, n_pool = cache_k.shape[:2]
    cache_kt = jnp.transpose(cache_k, (0, 1, 3, 4, 2)).reshape(n_a, n_pool, 2 * H_A * DH_A, PAGE_SIZE)
    cache_v2 = cache_v.reshape(n_a, n_pool, PAGE_SIZE * H_A, DV_A)
    y_s, o_s = _trunk(grp_s, x_s, pos_s, P, states, (cache_kt, cache_v2, page_table), tiles_s)

    from_tm = lambda a, w: jnp.moveaxis(a.reshape(w, db, a.shape[-1]), 0, 1)
    dff = f_w_dw.shape[-1]
    n_keep = CONV_W - 1
    outs = (
        y_p,
        from_tm(y_s, t_new),
        jnp.stack(o_p['k']).reshape(-1, bsz, seq, 2 * H_A, DH_A),
        jnp.stack(o_p['v']).reshape(-1, bsz, seq, H_A, DV_A),
        jnp.stack(o_p['wkv']),
        jnp.stack([h[:, -1] for h in o_p['shift']]),
        jnp.stack([c[:, -n_keep:] for c in o_p['conv']]),
        jnp.stack([f[:, -(FFN_CONV_W - 1):] for f in o_p['ffn']]),
        jnp.stack([from_tm(k, t_new) for k in o_s['k']]).reshape(-1, db, t_new, 2 * H_A, DH_A),
        jnp.stack([from_tm(v, t_new) for v in o_s['v']]).reshape(-1, db, t_new, H_A, DV_A),
        jnp.stack(o_s['wkv']),
        jnp.stack([h[0] for h in o_s['shift']]),
        jnp.stack([from_tm(jnp.concatenate([st.reshape(n_keep, db, d), glu.reshape(t_new, db, d)], axis=0)
                           [-n_keep:].reshape(-1, d), n_keep)
                   for st, glu in zip(states['conv'], o_s['conv'])]),
        jnp.stack([from_tm(f, FFN_CONV_W - 1) for f in o_s['ffn']]),
    )
    return outs
```

```python
import functools
import math

import numpy as np
import jax
import jax.numpy as jnp
from jax import lax
from jax.experimental import pallas as pl
from jax.experimental.pallas import tpu as pltpu

F32 = jnp.float32
BF16 = jnp.bfloat16

N_MIXERS = 3
H_A = 8
DH_A = 64
DV_A = 128
ROPE_THETA = 10000.0
PAGE_SIZE = 128
NEG_INF = -1e30
HEAD_B = 64
GN_EPS_B = 64e-5
LN_EPS = 1e-5
RMS_EPS = 1e-6
CONV_W = 31
FFN_CONV_W = 3

SUBLANES = 8
LANES = 128
VMEM_LIMIT_BYTES = 56 * 1024 * 1024

ATTN_RC = 256
ATTN_KC = 256
ATTN_DEPTH = 4
VT_ROWS = DV_A + 16
WKV_SEQS_PER_STEP = 4


def _cparams(*sem):
    return pltpu.CompilerParams(dimension_semantics=sem, vmem_limit_bytes=VMEM_LIMIT_BYTES)


def _resident(shape):
    nd = len(shape)
    return pl.BlockSpec(shape, lambda *_: (0,) * nd, pipeline_mode=pl.Buffered(1))


def _dot(a, b):
    return jnp.dot(a, b, preferred_element_type=F32)


def _dot_nt(a, b):
    return lax.dot_general(a, b, (((1,), (1,)), ((), ())), preferred_element_type=F32)


def _dot_tn(a, b):
    return lax.dot_general(a, b, (((0,), (0,)), ((), ())), preferred_element_type=F32)


def _sigmoid(x):
    return 1.0 / (1.0 + jnp.exp(-x))


def _silu(x):
    return x * _sigmoid(x)


def _bc_rows(m, tm):
    nb, c = m.shape
    if nb == 1 or nb == tm:
        return m
    return jnp.broadcast_to(m[None], (tm // nb, nb, c)).reshape(tm, c)


def _modnorm(x, gain, shift, scale):
    tm = x.shape[0]
    y = x * lax.rsqrt(jnp.mean(x * x, axis=-1, keepdims=True) + RMS_EPS) * gain
    return y * (1.0 + _bc_rows(scale, tm)) + _bc_rows(shift, tm)


class _Group:
    def __init__(self, nb, rows, sd, mod, has_state):
        self.nb = nb
        self.rows = rows
        self.sd = sd
        self.mod = mod
        self.has_state = has_state

    def mod_spec(self, layer, k, d):
        if self.mod.ndim == 4:
            return pl.BlockSpec((None, None, 1, d), lambda b, j: (layer, b, 0, k))
        return pl.BlockSpec((None, self.mod.shape[1], d), lambda b, j: (layer, 0, k))

    def row_spec(self, tm, c):
        return pl.BlockSpec((None, tm, c), lambda b, j: (b, j, 0))


def _adaln_body(c_ref, w_ref, b_ref, o_ref):
    sc = _silu(c_ref[...]).astype(BF16)
    o_ref[...] = _dot(sc, w_ref[...].astype(BF16)) + b_ref[...]


def _adaln(c_all, w_ada, b_ada):
    depth, d, n = w_ada.shape
    bp = c_all.shape[0]
    tn = 1536
    return pl.pallas_call(
        _adaln_body,
        grid=(depth, n // tn),
        in_specs=[pl.BlockSpec((bp, d), lambda i, j: (0, 0)),
                  pl.BlockSpec((None, d, tn), lambda i, j: (i, 0, j)),
                  pl.BlockSpec((None, 1, tn), lambda i, j: (i, 0, j))],
        out_specs=pl.BlockSpec((None, bp, tn), lambda i, j: (i, 0, j)),
        out_shape=jax.ShapeDtypeStruct((depth, bp, n), F32),
        compiler_params=_cparams("arbitrary", "arbitrary"),
        name="adaln",
    )(c_all, w_ada, b_ada.reshape(depth, 1, n))


def _pre_a_body(x_ref, sh_ref, sc_ref, gain_ref, w_ref, cos_ref, sina_ref, sinb_ref,
                k_out, v_out, q_out, kh_out, vh_out, *, head_major, q_scale):
    h = _modnorm(x_ref[...], gain_ref[...], sh_ref[...], sc_ref[...]).astype(BF16)
    qkv = _dot(h, w_ref[...])
    cos, sina, sinb = cos_ref[...], sina_ref[...], sinb_ref[...]
    nqk = H_A * LANES
    for s in range(2 * H_A):
        xs = qkv[:, LANES * s:LANES * (s + 1)]
        ro = xs * cos + pltpu.roll(xs, LANES - 32, 1) * sina + pltpu.roll(xs, 32, 1) * sinb
        if s < H_A:
            if head_major:
                q_out[s] = (ro * q_scale).astype(BF16)
            else:
                q_out[:, LANES * s:LANES * (s + 1)] = ro * q_scale
        else:
            hh = s - H_A
            k_out[:, LANES * hh:LANES * (hh + 1)] = ro
            if head_major:
                kh_out[hh] = ro.astype(BF16)
    v = qkv[:, 2 * nqk:]
    v_out[...] = v
    if head_major:
        for hh in range(H_A):
            vh_out[hh, 0:DV_A, :] = v[:, LANES * hh:LANES * (hh + 1)].T.astype(BF16)
            vh_out[hh, DV_A:VT_ROWS, :] = jnp.ones((VT_ROWS - DV_A, v.shape[0]), BF16)


def _rope_tables(pos):
    half = DH_A // 2
    inv = jnp.exp(-math.log(ROPE_THETA) * jnp.arange(half, dtype=F32) * (2.0 / DH_A))
    ang = pos.astype(F32)[:, None] * inv[None, :]
    cos, sin = jnp.cos(ang), jnp.sin(ang)
    zero = jnp.zeros_like(sin)
    cos_t = jnp.concatenate([cos, cos, cos, cos], axis=1)
    sina = jnp.concatenate([-sin, zero, -sin, zero], axis=1)
    sinb = jnp.concatenate([zero, sin, zero, sin], axis=1)
    return cos_t, sina, sinb


def _pre_a(grp, layer, x, gain, w_qkv, tabs, tm, head_major):
    nb, rows, d = x.shape
    n = w_qkv.shape[1]
    dk = H_A * LANES
    q_scale = DH_A ** -0.5 * (math.log2(math.e) if head_major else 1.0)
    kern = functools.partial(_pre_a_body, head_major=head_major, q_scale=q_scale)
    tab_spec = pl.BlockSpec((tm, LANES), lambda b, j: (j, 0))
    in_specs = [grp.row_spec(tm, d), grp.mod_spec(layer, 0, d), grp.mod_spec(layer, 1, d),
                _resident((1, d)), _resident((d, n)), tab_spec, tab_spec, tab_spec]
    out_specs = [grp.row_spec(tm, dk), grp.row_spec(tm, dk)]
    out_shape = [jax.ShapeDtypeStruct((nb, rows, dk), F32), jax.ShapeDtypeStruct((nb, rows, dk), F32)]
    if head_major:
        hspec = pl.BlockSpec((None, H_A, tm, LANES), lambda b, j: (b, 0, j, 0))
        tspec = pl.BlockSpec((None, H_A, VT_ROWS, tm), lambda b, j: (b, 0, 0, j))
        out_specs += [hspec, hspec, tspec]
        out_shape += [jax.ShapeDtypeStruct((nb, H_A, rows, LANES), BF16)] * 2
        out_shape += [jax.ShapeDtypeStruct((nb, H_A, VT_ROWS, rows), BF16)]
    else:
        dummy = pl.BlockSpec((None, SUBLANES, LANES), lambda b, j: (b, 0, 0))
        out_specs += [grp.row_spec(tm, dk), dummy, dummy]
        out_shape += [jax.ShapeDtypeStruct((nb, rows, dk), F32),
                      jax.ShapeDtypeStruct((nb, SUBLANES, LANES), BF16),
                      jax.ShapeDtypeStruct((nb, SUBLANES, LANES), BF16)]
    return pl.pallas_call(
        kern, grid=(nb, rows // tm), in_specs=in_specs, out_specs=out_specs, out_shape=out_shape,
        compiler_params=_cparams("arbitrary", "arbitrary"), name="pre_attn",
    )(x, grp.mod, grp.mod, gain, w_qkv, *tabs)


def _lam(lq1, lk1, lq2, lk2, lam_init):
    return (jnp.exp(jnp.sum(lq1 * lk1, axis=-1, keepdims=True))
            - jnp.exp(jnp.sum(lq2 * lk2, axis=-1, keepdims=True)) + lam_init)


def _diff_finish(acc, l, lam, g, lam_init, n):
    o = acc[0:n] / l[0:n] - lam * (acc[n:2 * n] / l[n:2 * n])
    return o * lax.rsqrt(jnp.mean(o * o, axis=-1, keepdims=True) + RMS_EPS) * g * (1.0 - lam_init)


def _attn_p_body(qi_tab, ki_tab, q_ref, k_ref, vt_ref, lq1, lk1, lq2, lk2, g_ref, o_ref,
                 qs_sc, m_sc, acc_sc, *, tq, rc, kc, depth, lam_init):
    t = pl.program_id(2)
    qi = qi_tab[t]
    ki = ki_tab[t]

    @pl.when(ki == 0)
    def _():
        q = q_ref[...]
        lane = lax.broadcasted_iota(jnp.int32, q.shape, 1)
        zero = jnp.zeros_like(q)
        qs_sc[0:tq, :] = jnp.where(lane < DH_A, q, zero)
        qs_sc[tq:2 * tq, :] = jnp.where(lane >= DH_A, q, zero)
        m_sc[...] = jnp.full(m_sc.shape, NEG_INF, F32)
        acc_sc[...] = jnp.zeros(acc_sc.shape, F32)

    def scores(blk):
        r0, k0, masked = blk
        st = _dot_nt(k_ref[k0:k0 + kc, :], qs_sc[r0:r0 + rc, :])
        if masked:
            kpos = lax.broadcasted_iota(jnp.int32, st.shape, 0) + k0
            qpos = lax.broadcasted_iota(jnp.int32, st.shape, 1) + (r0 % tq)
            st = jnp.where(kpos <= qpos, st, NEG_INF)
        return st

    def softmax_pv(blk, st):
        r0, k0, _ = blk
        cols = slice(r0, r0 + rc)
        m_prev = m_sc[:, cols]
        m_new = jnp.maximum(m_prev, jnp.max(st, axis=0, keepdims=True))
        alpha = jnp.exp2(m_prev - m_new)
        pt = jnp.exp2(st - m_new).astype(BF16)
        m_sc[:, cols] = m_new
        acc_sc[:, cols] = alpha * acc_sc[:, cols] + _dot(vt_ref[:, k0:k0 + kc], pt)

    def run(blocks):
        pending = []
        for blk in blocks:
            pending.append((blk, scores(blk)))
            if len(pending) > depth:
                softmax_pv(*pending.pop(0))
        for item in pending:
            softmax_pv(*item)

    @pl.when(ki < qi)
    def _():
        run([(r0, k0, False) for k0 in range(0, tq, kc) for r0 in range(0, 2 * tq, rc)])

    @pl.when(ki == qi)
    def _():
        blocks = []
        for k0 in range(0, tq, kc):
            for r0 in range(0, 2 * tq, rc):
                q_lo = r0 % tq
                if k0 <= q_lo + rc - 1:
                    blocks.append((r0, k0, k0 + kc - 1 > q_lo))
        run(blocks)
        lam = _lam(lq1[...], lk1[...], lq2[...], lk2[...], lam_init)
        l_tot = acc_sc[DV_A:DV_A + 1, :]
        acc = acc_sc[0:DV_A, :]
        ot = acc[:, 0:tq] / l_tot[:, 0:tq] - lam * (acc[:, tq:2 * tq] / l_tot[:, tq:2 * tq])
        ot = (ot * lax.rsqrt(jnp.mean(ot * ot, axis=0, keepdims=True) + RMS_EPS)
              * g_ref[...] * (1.0 - lam_init))
        o_ref[...] = ot.T.astype(BF16)


def _attn_prompt(qh, kh, vth, lq1, lk1, lq2, lk2, g, lam_init, tq):
    nb, nh, s, _ = qh.shape
    nq = s // tq
    rc = min(ATTN_RC, tq)
    kc = min(ATTN_KC, tq)
    pairs = [(qi, ki) for qi in range(nq) for ki in range(qi + 1)]
    qi_tab = jnp.asarray(np.array([p[0] for p in pairs], np.int32))
    ki_tab = jnp.asarray(np.array([p[1] for p in pairs], np.int32))
    small = pl.BlockSpec((1, DH_A), lambda b, h, t, qt, kt: (0, 0))
    grid_spec = pltpu.PrefetchScalarGridSpec(
        num_scalar_prefetch=2,
        grid=(nb, nh, len(pairs)),
        in_specs=[pl.BlockSpec((None, None, tq, LANES), lambda b, h, t, qt, kt: (b, h, qt[t], 0)),
                  pl.BlockSpec((None, None, tq, LANES), lambda b, h, t, qt, kt: (b, h, kt[t], 0)),
                  pl.BlockSpec((None, None, VT_ROWS, tq), lambda b, h, t, qt, kt: (b, h, 0, kt[t])),
                  small, small, small, small,
                  pl.BlockSpec((DV_A, 1), lambda b, h, t, qt, kt: (0, 0))],
        out_specs=pl.BlockSpec((None, tq, LANES), lambda b, h, t, qt, kt: (b, qt[t], h)),
        scratch_shapes=[pltpu.VMEM((2 * tq, LANES), BF16), pltpu.VMEM((1, 2 * tq), F32),
                        pltpu.VMEM((VT_ROWS, 2 * tq), F32)])
    return pl.pallas_call(
        functools.partial(_attn_p_body, tq=tq, rc=rc, kc=kc, depth=ATTN_DEPTH, lam_init=lam_init),
        grid_spec=grid_spec,
        out_shape=jax.ShapeDtypeStruct((nb, s, nh * LANES), BF16),
        compiler_params=_cparams("arbitrary", "arbitrary", "arbitrary"), name="attn_prompt",
    )(qi_tab, ki_tab, qh, kh, vth, lq1, lk1, lq2, lk2, g)


def _attn_s_body(pt, q_ref, kn_ref, vn_ref, *rest, n_pages, lam_init):
    k_refs = rest[:n_pages]
    v_refs = rest[n_pages:2 * n_pages]
    lq1, lk1, lq2, lk2, g_ref, o_ref = rest[2 * n_pages:]
    t_new, dk = q_ref.shape
    n_sub = 2 * H_A
    n_rows = n_sub * t_new
    page = k_refs[0].shape[1]

    q_tiled = jnp.concatenate([q_ref[...]] * n_sub, axis=0)
    ri = lax.broadcasted_iota(jnp.int32, (n_rows, dk), 0)
    ci = lax.broadcasted_iota(jnp.int32, (n_rows, dk), 1)
    same_sub = (lax.shift_right_logical(ri, int(math.log2(t_new)))
                == lax.shift_right_logical(ci, int(math.log2(DH_A))))
    qbd = jnp.where(same_sub, q_tiled, jnp.zeros_like(q_tiled)).astype(BF16)

    s_past = jnp.concatenate([_dot(qbd, k_refs[p][...].astype(BF16)) for p in range(n_pages)], axis=1)
    s_new = _dot_nt(qbd, kn_ref[...].astype(BF16))
    row = lax.broadcasted_iota(jnp.int32, s_new.shape, 0)
    col = lax.broadcasted_iota(jnp.int32, s_new.shape, 1)
    s_new = jnp.where(col <= jnp.bitwise_and(row, t_new - 1), s_new, NEG_INF)
    m = jnp.maximum(jnp.max(s_past, axis=1, keepdims=True), jnp.max(s_new, axis=1, keepdims=True))
    p_past = jnp.exp(s_past - m)
    p_new = jnp.exp(s_new - m)
    l = jnp.sum(p_past, axis=1, keepdims=True) + jnp.sum(p_new, axis=1, keepdims=True)
    p_past = p_past.astype(BF16)

    lam = _lam(lq1[...], lk1[...], lq2[...], lk2[...], lam_init)
    vn = vn_ref[...]
    for hh in range(H_A):
        rows = slice(2 * t_new * hh, 2 * t_new * (hh + 1))
        v_h = jnp.concatenate([v_refs[p][pl.ds(hh, page, stride=H_A), :] for p in range(n_pages)],
                              axis=0).astype(BF16)
        acc = _dot(p_past[rows, :], v_h) + _dot(p_new[rows, :], vn[:, LANES * hh:LANES * (hh + 1)])
        o = _diff_finish(acc, l[rows, :], lam, g_ref[...], lam_init, t_new)
        o_ref[:, LANES * hh:LANES * (hh + 1)] = o.astype(BF16)


def _attn_sample(q, k_new, v_new, cache_kt, cache_v2, page_table, j, lq1, lk1, lq2, lk2, g, lam_init):
    t_new, width = q.shape
    db, n_pages = page_table.shape
    dk = width // db
    small = pl.BlockSpec((1, DH_A), lambda b, pt: (0, 0))
    seq = pl.BlockSpec((t_new, dk), lambda b, pt: (0, b))

    def page_spec(arr, p):
        return pl.BlockSpec((None, None) + arr.shape[2:], lambda b, pt: (j, pt[b, p], 0, 0))

    grid_spec = pltpu.PrefetchScalarGridSpec(
        num_scalar_prefetch=1,
        grid=(db,),
        in_specs=([seq, seq, seq] + [page_spec(cache_kt, p) for p in range(n_pages)]
                  + [page_spec(cache_v2, p) for p in range(n_pages)]
                  + [small, small, small, small, pl.BlockSpec((1, DV_A), lambda b, pt: (0, 0))]),
        out_specs=seq)
    return pl.pallas_call(
        functools.partial(_attn_s_body, n_pages=n_pages, lam_init=lam_init),
        grid_spec=grid_spec,
        out_shape=jax.ShapeDtypeStruct((t_new, width), BF16),
        compiler_params=_cparams("arbitrary"), name="attn_sample",
    )(page_table, q, k_new, v_new, *([cache_kt] * n_pages), *([cache_v2] * n_pages),
      lq1, lk1, lq2, lk2, g)


def _pre_b_body(*refs, sd, halo, has_state):
    (x_ref, sh_ref, sc_ref, gain_ref, mu_ref, wr, wk, wv, w0, w1, w2, a0, a1, a2, g1, g2) = refs[:16]
    rest = refs[16:]
    if has_state:
        st_ref, rest = rest[0], rest[1:]
    r_out, wl_out, k_out, v_out, a_out, g_out, hl_out, ext = rest
    j = pl.program_id(1)
    tm = x_ref.shape[0]

    @pl.when(j == 0)
    def _():
        if has_state:
            ext[0:halo, :] = st_ref[...]
        else:
            ext[0:halo, :] = jnp.zeros((halo, ext.shape[1]), F32)

    h = _modnorm(x_ref[...], gain_ref[...], sh_ref[...], sc_ref[...])
    ext[halo:halo + tm, :] = h
    xx = ext[pl.ds(halo - sd, tm), :] - h
    ext[0:halo, :] = ext[tm:tm + halo, :]
    hl_out[...] = h[tm - halo:tm]

    mu = mu_ref[...]
    mix = lambda n: (h + xx * mu[n:n + 1]).astype(BF16)
    r_out[...] = _dot(mix(0), wr[...])
    wraw = w0[...] + _dot(jnp.tanh(_dot(mix(1), w1[...])).astype(BF16), w2[...])
    k_out[...] = _dot(mix(2), wk[...])
    v_out[...] = _dot(mix(3), wv[...])
    a_out[...] = _sigmoid(a0[...] + _dot(_dot(mix(4), a1[...]).astype(BF16), a2[...]))
    g_out[...] = _dot(_sigmoid(_dot(mix(5), g1[...])).astype(BF16), g2[...])
    z = -wraw
    softplus = jnp.maximum(z, 0.0) + jnp.log(1.0 + jnp.exp(-jnp.abs(z)))
    wl_out[...] = -jnp.exp(-softplus - 0.5)


def _pre_b(grp, layer, x, gain, prm, state, tm):
    nb, rows, d = x.shape
    halo = max(SUBLANES, grp.sd)
    kern = functools.partial(_pre_b_body, sd=grp.sd, halo=halo, has_state=grp.has_state)
    row = lambda a: a.reshape(1, -1)
    ins = [x, grp.mod, grp.mod, gain, prm['mu'], prm['w_r'], prm['w_k'], prm['w_v'], row(prm['w0']),
           prm['w1'], prm['w2'], row(prm['a0']), prm['a1'], prm['a2'], prm['g1'], prm['g2']]
    in_specs = [grp.row_spec(tm, d), grp.mod_spec(layer, 0, d), grp.mod_spec(layer, 1, d)]
    in_specs += [_resident(a.shape) for a in ins[3:]]
    if grp.has_state:
        ins.append(state)
        in_specs.append(_resident(state.shape))
    big = jax.ShapeDtypeStruct((nb, rows, d), F32)
    out_specs = [grp.row_spec(tm, d)] * 6 + [pl.BlockSpec((None, halo, d), lambda b, j: (b, 0, 0))]
    out_shape = [big] * 6 + [jax.ShapeDtypeStruct((nb, halo, d), F32)]
    return pl.pallas_call(
        kern, grid=(nb, rows // tm), in_specs=in_specs, out_specs=out_specs, out_shape=out_shape,
        scratch_shapes=[pltpu.VMEM((halo + tm, d), F32)],
        compiler_params=_cparams("arbitrary", "arbitrary"), name="pre_rwkv",
    )(*ins)


def _wkv_body(*refs, L, has_state, mm_dtype):
    (r_ref, wl_ref, k_ref, v_ref, a_ref, g_ref, kk_ref, ka_ref, rk_ref, lng_ref, lnb_ref) = refs[:11]
    rest = refs[11:]
    if has_state:
        s0_ref, rest = rest[0], rest[1:]
    mo_ref, sout_ref, g_sc = rest
    c = pl.program_id(1)
    d = kk_ref.shape[1]
    batched = len(r_ref.shape) == 3
    nbb = r_ref.shape[0] if batched else r_ref.shape[1] // d
    n_pairs = d // LANES
    hb = HEAD_B
    mm = lambda x: x.astype(mm_dtype)
    seq = lambda ref: ([ref[bb] for bb in range(nbb)] if batched
                       else [ref[:, d * bb:d * (bb + 1)] for bb in range(nbb)])
    units = [(bb, pr) for bb in range(nbb) for pr in range(n_pairs)]

    @pl.when(c == 0)
    def _():
        if has_state:
            zero = jnp.zeros((hb, hb), F32)
            for i, (bb, pr) in enumerate(units):
                top = jnp.concatenate([s0_ref[bb, 2 * pr], zero], axis=1)
                bot = jnp.concatenate([zero, s0_ref[bb, 2 * pr + 1]], axis=1)
                g_sc[i] = jnp.concatenate([top, bot], axis=0)
        else:
            g_sc[...] = jnp.zeros(g_sc.shape, F32)

    ti = lax.broadcasted_iota(jnp.int32, (L, L), 0)
    tj = lax.broadcasted_iota(jnp.int32, (L, L), 1)
    tri = jnp.where(tj <= ti, 1.0, 0.0).astype(F32)

    def cumsum_time(w):
        if mm_dtype == F32:
            return jnp.dot(tri, w, preferred_element_type=F32, precision=lax.Precision.HIGHEST)
        tri_b = tri.astype(BF16)
        hi = w.astype(BF16)
        rem = w - hi.astype(F32)
        mid = rem.astype(BF16)
        lo = (rem - mid.astype(F32)).astype(BF16)
        return _dot(tri_b, hi) + _dot(tri_b, mid) + _dot(tri_b, lo)

    wl_b = seq(wl_ref)
    cum_b = [cumsum_time(w) for w in wl_b]
    gam_b = [jnp.exp(cm) for cm in cum_b]
    gam_prev_b = [jnp.exp(cum_b[bb] - wl_b[bb]) for bb in range(nbb)]
    inv_gam_b = [jnp.exp(-cm) for cm in cum_b]

    lane = lax.broadcasted_iota(jnp.int32, (L, LANES), 1)
    in_h0 = lane < hb
    bi = lax.broadcasted_iota(jnp.int32, (LANES, LANES), 0)
    bj = lax.broadcasted_iota(jnp.int32, (LANES, LANES), 1)
    same_head = (bi < hb) == (bj < hb)
    ones_bd = jnp.where(same_head, 1.0, 0.0).astype(BF16)

    def headsum(x):
        return _dot(x.astype(BF16), ones_bd)

    def stack(x):
        zero = jnp.zeros_like(x)
        return jnp.concatenate([jnp.where(in_h0, x, zero), jnp.where(in_h0, zero, x)], axis=0)

    wr = lax.broadcasted_iota(jnp.int32, (L, 2 * L), 0)
    wc = lax.broadcasted_iota(jnp.int32, (L, 2 * L), 1)
    wt = jnp.where(wc >= L, wc - L, wc)
    strict = wt < wr
    incl = wt <= wr
    w_h0 = wc < L

    def bd(w):
        zero = jnp.zeros_like(w)
        return jnp.concatenate([jnp.where(w_h0, w, zero), jnp.where(w_h0, zero, w)], axis=0)

    r_b, k_b, v_b, a_b, g_b = seq(r_ref), seq(k_ref), seq(v_ref), seq(a_ref), seq(g_ref)
    prs = range(len(units))
    sls = [slice(LANES * pr, LANES * (pr + 1)) for _, pr in units]
    per_unit = lambda arrs: [arrs[bb][:, sls[i]] for i, (bb, _) in enumerate(units)]
    a, kraw, v, r, g_gate = per_unit(a_b), per_unit(k_b), per_unit(v_b), per_unit(r_b), per_unit(g_b)
    gam, gam_prev, inv_gam = per_unit(gam_b), per_unit(gam_prev_b), per_unit(inv_gam_b)
    kk = [kraw[i] * kk_ref[:, sls[i]] for i in prs]
    kmod = [kraw[i] * (1.0 + (a[i] - 1.0) * ka_ref[:, sls[i]]) for i in prs]
    ssq = [headsum(kk[i] * kk[i]) for i in prs]
    bonus_w = [headsum(r[i] * kmod[i] * rk_ref[:, sls[i]]) for i in prs]
    kk = [kk[i] * lax.rsqrt(jnp.maximum(ssq[i], 1e-24)) for i in prs]
    P = [-kk[i] * gam_prev[i] for i in prs]
    Q = [kk[i] * a[i] * inv_gam[i] for i in prs]
    K = [kmod[i] * inv_gam[i] for i in prs]
    R = [r[i] * gam[i] for i in prs]

    z = [_dot_nt(mm(jnp.concatenate([P[i], R[i]], axis=0)),
                 mm(jnp.concatenate([stack(Q[i]), stack(K[i])], axis=0))) for i in prs]
    zero_w = jnp.zeros((L, 2 * L), F32)
    a_pq = [jnp.where(strict, z[i][0:L, 0:2 * L], zero_w) for i in prs]
    a_pk = [jnp.where(strict, z[i][0:L, 2 * L:4 * L], zero_w) for i in prs]
    a_rq = [jnp.where(incl, z[i][L:2 * L, 0:2 * L], zero_w) for i in prs]
    a_rk = [jnp.where(incl, z[i][L:2 * L, 2 * L:4 * L], zero_w) for i in prs]

    nmat = list(a_pq)
    pw = list(a_pq)
    for _ in range(int(math.log2(L)) - 1):
        pw = [_dot(mm(pw[i]), mm(bd(pw[i]))) for i in prs]
        nmat = [nmat[i] + pw[i] + _dot(mm(nmat[i]), mm(bd(pw[i]))) for i in prs]

    gmat = [g_sc[i] for i in prs]
    gb = [mm(gmat[i]) for i in prs]
    sv = [mm(stack(v[i])) for i in prs]
    rhs = [_dot_nt(mm(P[i]), gb[i]) + _dot(mm(a_pk[i]), sv[i]) for i in prs]
    u = [rhs[i] + _dot(mm(nmat[i]), mm(stack(rhs[i]))) for i in prs]
    y = [_dot_nt(mm(R[i]), gb[i]) + _dot(mm(a_rq[i]), mm(stack(u[i]))) + _dot(mm(a_rk[i]), sv[i])
         for i in prs]
    dg = [_dot_tn(mm(jnp.concatenate([u[i], v[i]], axis=0)), mm(jnp.concatenate([Q[i], K[i]], axis=0)))
          for i in prs]
    for i in prs:
        g_sc[i] = (gmat[i] + jnp.where(same_head, dg[i], jnp.zeros_like(dg[i]))) * gam[i][L - 1:L]

    mean = [headsum(y[i]) * (1.0 / hb) for i in prs]
    dlt = [y[i] - mean[i] for i in prs]
    var = [headsum(dlt[i] * dlt[i]) * (1.0 / hb) for i in prs]
    for i, (bb, _) in enumerate(units):
        yn = dlt[i] * lax.rsqrt(var[i] + GN_EPS_B) * lng_ref[:, sls[i]] + lnb_ref[:, sls[i]]
        out = ((yn + bonus_w[i] * v[i]) * g_gate[i]).astype(BF16)
        if batched:
            mo_ref[bb, :, sls[i]] = out
        else:
            mo_ref[:, d * bb + sls[i].start:d * bb + sls[i].stop] = out

    @pl.when(c == pl.num_programs(1) - 1)
    def _():
        for i, (bb, pr) in enumerate(units):
            gfin = g_sc[i]
            sout_ref[bb, 2 * pr] = gfin[0:hb, 0:hb]
            sout_ref[bb, 2 * pr + 1] = gfin[hb:2 * hb, hb:2 * hb]


def _wkv(seqs, prm, state, L, time_major):
    row = lambda a: a.reshape(1, -1)
    d = prm['k_k'].shape[-1]
    n_heads = d // HEAD_B
    if time_major:
        t_len, width = seqs[0].shape
        nb = width // d
        nbb = WKV_SEQS_PER_STEP
        seq_spec = pl.BlockSpec((L, nbb * d), lambda b, c: (c, b))
        mo_shape = jax.ShapeDtypeStruct((t_len, width), BF16)
    else:
        nb, t_len, _ = seqs[0].shape
        nbb = nb
        seq_spec = pl.BlockSpec((nbb, L, d), lambda b, c: (b, c, 0))
        mo_shape = jax.ShapeDtypeStruct((nb, t_len, d), BF16)
    has_state = state is not None
    st_spec = pl.BlockSpec((nbb, n_heads, HEAD_B, HEAD_B), lambda b, c: (b, 0, 0, 0))
    ins = list(seqs) + [row(prm['k_k']), row(prm['k_a']), row(prm['r_k']), row(prm['ln_g']), row(prm['ln_b'])]
    in_specs = [seq_spec] * 6 + [_resident((1, d))] * 5
    if has_state:
        ins.append(state)
        in_specs.append(st_spec)
    kern = functools.partial(_wkv_body, L=L, has_state=has_state, mm_dtype=BF16 if L >= 16 else F32)
    return pl.pallas_call(
        kern, grid=(nb // nbb, t_len // L), in_specs=in_specs,
        out_specs=[seq_spec, st_spec],
        out_shape=[mo_shape, jax.ShapeDtypeStruct((nb, n_heads, HEAD_B, HEAD_B), F32)],
        scratch_shapes=[pltpu.VMEM((nbb * n_heads // 2, LANES, LANES), F32)],
        compiler_params=_cparams("arbitrary", "arbitrary"), name="wkv",
    )(*ins)


def _layernorm_silu(y, g, b):
    mu = jnp.mean(y, axis=-1, keepdims=True)
    dlt = y - mu
    var = jnp.mean(dlt * dlt, axis=-1, keepdims=True)
    return _silu(dlt * lax.rsqrt(var + LN_EPS) * g + b)


def _pre_c_prompt_body(x_ref, sh_ref, sc_ref, gain_ref, w1, b1, wdw, bdw, lng, lnb,
                       mo_ref, cache_ref, ext, shift_sc, *, halo):
    j = pl.program_id(1)
    tm, d = x_ref.shape

    @pl.when(j == 0)
    def _():
        ext[0:halo, :] = jnp.zeros((halo, d), F32)

    h = _modnorm(x_ref[...], gain_ref[...], sh_ref[...], sc_ref[...]).astype(BF16)
    u = _dot(h, w1[...]) + b1[...]
    glu = u[:, :d] * _sigmoid(u[:, d:])
    ext[halo:halo + tm, :] = glu
    span = halo - SUBLANES
    cols_out = []
    for lt in range(d // LANES):
        cols = slice(LANES * lt, LANES * (lt + 1))
        acc = None
        for b in range(SUBLANES):
            shift_sc[b] = ext[pl.ds(SUBLANES - b, tm + span), cols]
            for a in range(halo // SUBLANES):
                dd = SUBLANES * a + b
                if dd >= CONV_W:
                    continue
                off = span - SUBLANES * a
                term = shift_sc[b, off:off + tm, :] * wdw[CONV_W - 1 - dd:CONV_W - dd, cols]
                acc = term if acc is None else acc + term
        cols_out.append(acc)
    y = jnp.concatenate(cols_out, axis=1) + bdw[...]
    tail = ext[tm:tm + halo, :]
    ext[0:halo, :] = tail
    cache_ref[...] = tail
    mo_ref[...] = _layernorm_silu(y, lng[...], lnb[...]).astype(BF16)


def _pre_c_sample_body(x_ref, sh_ref, sc_ref, gain_ref, w1, b1, wdw, bdw, lng, lnb, st_ref,
                       mo_ref, glu_ref, *, nb):
    tm, d = x_ref.shape
    n_t = tm // nb
    n_st = CONV_W - 1
    h = _modnorm(x_ref[...], gain_ref[...], sh_ref[...], sc_ref[...]).astype(BF16)
    u = _dot(h, w1[...]) + b1[...]
    glu = u[:, :d] * _sigmoid(u[:, d:])
    glu_ref[...] = glu
    for t in range(n_t):
        acc = None
        for w in range(CONV_W):
            pos = t + w
            if pos < n_st:
                slab = st_ref[nb * pos:nb * (pos + 1), :]
            else:
                slab = glu[nb * (pos - n_st):nb * (pos - n_st + 1), :]
            term = slab * wdw[w:w + 1, :]
            acc = term if acc is None else acc + term
        mo_ref[nb * t:nb * (t + 1), :] = _layernorm_silu(acc + bdw[...], lng[...], lnb[...]).astype(BF16)


def _pre_c(grp, layer, x, gain, prm, state, tm):
    nb, rows, d = x.shape
    row = lambda a: a.reshape(1, -1)
    ins = [x, grp.mod, grp.mod, gain, prm['w_pw1'], row(prm['b_pw1']), prm['w_dw'], row(prm['b_dw']),
           row(prm['ln_g']), row(prm['ln_b'])]
    in_specs = [grp.row_spec(tm, d), grp.mod_spec(layer, 0, d), grp.mod_spec(layer, 1, d)]
    in_specs += [_resident(a.shape) for a in ins[3:]]
    if grp.has_state:
        db = grp.sd
        ins.append(state)
        in_specs.append(_resident(state.shape))
        return pl.pallas_call(
            functools.partial(_pre_c_sample_body, nb=db), grid=(nb, 1), in_specs=in_specs,
            out_specs=[grp.row_spec(rows, d), grp.row_spec(rows, d)],
            out_shape=[jax.ShapeDtypeStruct((nb, rows, d), BF16), jax.ShapeDtypeStruct((nb, rows, d), F32)],
            compiler_params=_cparams("arbitrary", "arbitrary"), name="pre_conv_sample",
        )(*ins)
    halo = 32
    return pl.pallas_call(
        functools.partial(_pre_c_prompt_body, halo=halo), grid=(nb, rows // tm), in_specs=in_specs,
        out_specs=[grp.row_spec(tm, d), pl.BlockSpec((None, halo, d), lambda b, j: (b, 0, 0))],
        out_shape=[jax.ShapeDtypeStruct((nb, rows, d), BF16), jax.ShapeDtypeStruct((nb, halo, d), F32)],
        scratch_shapes=[pltpu.VMEM((halo + tm, d), F32),
                        pltpu.VMEM((SUBLANES, tm + halo - SUBLANES, LANES), F32)],
        compiler_params=_cparams("arbitrary", "arbitrary"), name="pre_conv_prompt",
    )(*ins)


def _post_ffn_body(*refs, sd, halo, has_state, final):
    (x_ref, mo_ref, g1_ref, sh_ref, sc_ref, g2_ref, wo, bo, gain_ref, w_in, wdw, bdw, w_out,
     nout_ref) = refs[:14]
    rest = refs[14:]
    if has_state:
        st_ref, rest = rest[0], rest[1:]
    y_ref, cache_ref, ext = rest
    j = pl.program_id(1)
    tm = x_ref.shape[0]
    dff = ext.shape[1]

    @pl.when(j == 0)
    def _():
        if has_state:
            ext[0:halo, :] = st_ref[...]
        else:
            ext[0:halo, :] = jnp.zeros((halo, dff), F32)

    x1 = x_ref[...] + _bc_rows(g1_ref[...], tm) * (_dot(mo_ref[...], wo[...]) + bo[...])
    h = _modnorm(x1, gain_ref[...], sh_ref[...], sc_ref[...]).astype(BF16)
    u = _dot(h, w_in[...])
    act = u[:, :dff]
    val = u[:, dff:]
    ext[halo:halo + tm, :] = act
    conv = (act * wdw[2:3, :] + ext[pl.ds(halo - sd, tm), :] * wdw[1:2, :]
            + ext[pl.ds(halo - 2 * sd, tm), :] * wdw[0:1, :] + bdw[...])
    tail = ext[tm:tm + halo, :]
    ext[0:halo, :] = tail
    cache_ref[...] = tail
    gated = (_silu(conv) * val).astype(BF16)
    x2 = x1 + _bc_rows(g2_ref[...], tm) * _dot(gated, w_out[...])
    if final:
        x2 = x2 * lax.rsqrt(jnp.mean(x2 * x2, axis=-1, keepdims=True) + RMS_EPS) * nout_ref[...]
    y_ref[...] = x2


def _post_ffn(grp, layer, x, mo, w_o, b_o, gain, prm, state, norm_out, tm, final):
    nb, rows, d = x.shape
    dff = prm['w_dw'].shape[1]
    halo = max(SUBLANES, (FFN_CONV_W - 1) * grp.sd)
    row = lambda a: a.reshape(1, -1)
    ins = [x, mo, grp.mod, grp.mod, grp.mod, grp.mod, w_o, row(b_o), gain, prm['w_in'], prm['w_dw'],
           row(prm['b_dw']), prm['w_out'], row(norm_out)]
    in_specs = [grp.row_spec(tm, d), grp.row_spec(tm, d), grp.mod_spec(layer, 2, d),
                grp.mod_spec(layer, 3, d), grp.mod_spec(layer, 4, d), grp.mod_spec(layer, 5, d)]
    in_specs += [_resident(a.shape) for a in ins[6:]]
    if grp.has_state:
        ins.append(state)
        in_specs.append(_resident(state.shape))
    kern = functools.partial(_post_ffn_body, sd=grp.sd, halo=halo, has_state=grp.has_state, final=final)
    return pl.pallas_call(
        kern, grid=(nb, rows // tm), in_specs=in_specs,
        out_specs=[grp.row_spec(tm, d), pl.BlockSpec((None, halo, dff), lambda b, j: (b, 0, 0))],
        out_shape=[jax.ShapeDtypeStruct((nb, rows, d), F32), jax.ShapeDtypeStruct((nb, halo, dff), F32)],
        scratch_shapes=[pltpu.VMEM((halo + tm, dff), F32)],
        compiler_params=_cparams("arbitrary", "arbitrary"), name="post_ffn",
    )(*ins)


def _trunk(grp, x, pos, P, states, paged, tiles):
    nb, rows, d = x.shape
    depth = P['norm_mix'].shape[0]
    out = dict(k=[], v=[], wkv=[], shift=[], conv=[], ffn=[])
    tabs = _rope_tables(pos)
    for i in range(depth):
        kind, j = i % N_MIXERS, i // N_MIXERS
        gain_mix = P['norm_mix'][i].reshape(1, d)
        zeros_d = jnp.zeros((d,), F32)
        if kind == 0:
            lam_init = 0.8 - 0.6 * math.exp(-0.3 * i)
            lvec = [P[n][j].reshape(1, DH_A) for n in ('a_lq1', 'a_lk1', 'a_lq2', 'a_lk2')]
            subln = P['a_subln'][j].reshape(1, DV_A)
            head_major = paged is None
            k, v, q, kh, vh = _pre_a(grp, i, x, gain_mix, P['a_w_qkv'][j], tabs, tiles['pre_a'], head_major)
            if paged is None:
                mo = _attn_prompt(q, kh, vh, *lvec, subln.reshape(DV_A, 1), lam_init, tiles['attn_q'])
            else:
                cache_k, cache_v, page_table = paged
                db = page_table.shape[0]
                t_new = rows // db
                tm_view = lambda a: a.reshape(t_new, db * a.shape[-1])
                mo = _attn_sample(tm_view(q), tm_view(k), tm_view(v), cache_k, cache_v, page_table, j,
                                  *lvec, subln, lam_init).reshape(nb, rows, d)
            out['k'].append(k)
            out['v'].append(v)
            w_o, b_o = P['a_w_o'][j], zeros_d
        elif kind == 1:
            prm = {n[2:]: P[n][j] for n in P if n.startswith('b_')}
            st_shift, st_wkv = (states['shift'][j], states['wkv'][j]) if grp.has_state else (None, None)
            r, wl, k, v, a, g, hl = _pre_b(grp, i, x, gain_mix, prm, st_shift, tiles['pre_b'])
            seqs = (r, wl, k, v, a, g)
            if grp.has_state:
                t_new = rows // grp.sd
                seqs = tuple(s.reshape(t_new, grp.sd * d) for s in seqs)
                mo, wkv = _wkv(seqs, prm, st_wkv, t_new, True)
                mo = mo.reshape(nb, rows, d)
            else:
                mo, wkv = _wkv(seqs, prm, None, tiles['wkv'], False)
            out['shift'].append(hl)
            out['wkv'].append(wkv)
            w_o, b_o = prm['w_o'], zeros_d
        else:
            prm = {n[3:]: P[n][j] for n in P if n.startswith('cv_')}
            st = states['conv'][j] if grp.has_state else None
            mo, cache = _pre_c(grp, i, x, gain_mix, prm, st, tiles['pre_c'])
            out['conv'].append(cache)
            w_o, b_o = prm['w_pw2'], prm['b_pw2']
        prm = {n[2:]: P[n][i] for n in ('f_w_in', 'f_w_dw', 'f_b_dw', 'f_w_out')}
        st = states['ffn'][i] if grp.has_state else None
        x, fcache = _post_ffn(grp, i, x, mo, w_o, b_o, P['norm_ffn'][i].reshape(1, d), prm, st,
                              P['norm_out'], tiles['ffn'], i == depth - 1)
        out['ffn'].append(fcache)
    return x, out


def kernel(x_prompt, x_sample, cache_k, cache_v, state_wkv, state_shift, state_conv, state_ffn, page_table, c_prompt, c_sample, norm_mix, norm_ffn, w_ada, b_ada, a_w_qkv, a_w_o, a_lq1, a_lk1, a_lq2, a_lk2, a_subln, b_mu, b_w_r, b_w_k, b_w_v, b_w_o, b_w0, b_w1, b_w2, b_a0, b_a1, b_a2, b_g1, b_g2, b_k_k, b_k_a, b_r_k, b_ln_g, b_ln_b, cv_w_pw1, cv_b_pw1, cv_w_dw, cv_b_dw, cv_ln_g, cv_ln_b, cv_w_pw2, cv_b_pw2, f_w_in, f_w_dw, f_b_dw, f_w_out, norm_out):
    bsz, seq, d = x_prompt.shape
    db, t_new, _ = x_sample.shape
    depth = norm_mix.shape[0]
    past = page_table.shape[1] * PAGE_SIZE
    bf = lambda a: a.astype(BF16)
    P = dict(norm_mix=norm_mix, norm_ffn=norm_ffn, norm_out=norm_out,
             a_w_qkv=bf(a_w_qkv), a_w_o=bf(a_w_o), a_lq1=a_lq1, a_lk1=a_lk1, a_lq2=a_lq2, a_lk2=a_lk2,
             a_subln=a_subln, b_mu=b_mu, b_w_r=bf(b_w_r), b_w_k=bf(b_w_k), b_w_v=bf(b_w_v),
             b_w_o=bf(b_w_o), b_w0=b_w0, b_w1=bf(b_w1), b_w2=bf(b_w2), b_a0=b_a0, b_a1=bf(b_a1),
             b_a2=bf(b_a2), b_g1=bf(b_g1), b_g2=bf(b_g2), b_k_k=b_k_k, b_k_a=b_k_a, b_r_k=b_r_k,
             b_ln_g=b_ln_g, b_ln_b=b_ln_b, cv_w_pw1=bf(cv_w_pw1), cv_b_pw1=cv_b_pw1, cv_w_dw=cv_w_dw,
             cv_b_dw=cv_b_dw, cv_ln_g=cv_ln_g, cv_ln_b=cv_ln_b, cv_w_pw2=bf(cv_w_pw2),
             cv_b_pw2=cv_b_pw2, f_w_in=bf(f_w_in), f_w_dw=f_w_dw, f_b_dw=f_b_dw, f_w_out=bf(f_w_out))

    pad = (-bsz) % SUBLANES
    c_all = jnp.concatenate([c_prompt, jnp.zeros((pad, d), F32), c_sample], axis=0)
    mod = _adaln(c_all, w_ada, b_ada)
    mod_p = mod[:, :bsz].reshape(depth, bsz, 1, 6 * d)
    mod_s = mod[:, bsz + pad:]

    grp_p = _Group(bsz, seq, 1, mod_p, False)
    tiles_p = dict(pre_a=min(512, seq), attn_q=min(2048, seq), pre_b=min(512, seq), wkv=64,
                   pre_c=min(256, seq), ffn=min(256, seq))
    y_p, o_p = _trunk(grp_p, x_prompt, jnp.arange(seq), P, None, None, tiles_p)

    rows_s = t_new * db
    to_tm = lambda a: jnp.moveaxis(a, 1, 0)
    x_s = to_tm(x_sample).reshape(1, rows_s, d)
    states = dict(
        shift=[state_shift[j] for j in range(state_shift.shape[0])],
        wkv=[state_wkv[j] for j in range(state_wkv.shape[0])],
        conv=[to_tm(state_conv[j]).reshape(-1, d) for j in range(state_conv.shape[0])],
        ffn=[to_tm(state_ffn[i]).reshape(-1, state_ffn.shape[-1]) for i in range(depth)])
    grp_s = _Group(1, rows_s, db, mod_s, True)
    tiles_s = dict(pre_a=rows_s, pre_b=rows_s // 2, pre_c=rows_s, ffn=2 * db)
    pos_s = jnp.repeat(past + jnp.arange(t_new), db)
    n_a, n_pool = cache_k.shape[:2]
    cache_kt = jnp.transpose(cache_k, (0, 1, 3, 4, 2)).reshape(n_a, n_pool, 2 * H_A * DH_A, PAGE_SIZE)
    cache_v2 = cache_v.reshape(n_a, n_pool, PAGE_SIZE * H_A, DV_A)
    y_s, o_s = _trunk(grp_s, x_s, pos_s, P, states, (cache_kt, cache_v2, page_table), tiles_s)

    from_tm = lambda a, w: jnp.moveaxis(a.reshape(w, db, a.shape[-1]), 0, 1)
    dff = f_w_dw.shape[-1]
    n_keep = CONV_W - 1
    outs = (
        y_p,
        from_tm(y_s, t_new),
        jnp.stack(o_p['k']).reshape(-1, bsz, seq, 2 * H_A, DH_A),
        jnp.stack(o_p['v']).reshape(-1, bsz, seq, H_A, DV_A),
        jnp.stack(o_p['wkv']),
        jnp.stack([h[:, -1] for h in o_p['shift']]),
        jnp.stack([c[:, -n_keep:] for c in o_p['conv']]),
        jnp.stack([f[:, -(FFN_CONV_W - 1):] for f in o_p['ffn']]),
        jnp.stack([from_tm(k, t_new) for k in o_s['k']]).reshape(-1, db, t_new, 2 * H_A, DH_A),
        jnp.stack([from_tm(v, t_new) for v in o_s['v']]).reshape(-1, db, t_new, H_A, DV_A),
        jnp.stack(o_s['wkv']),
        jnp.stack([h[0] for h in o_s['shift']]),
        jnp.stack([from_tm(jnp.concatenate([st.reshape(n_keep, db, d), glu.reshape(t_new, db, d)], axis=0)
                           [-n_keep:].reshape(-1, d), n_keep)
                   for st, glu in zip(states['conv'], o_s['conv'])]),
        jnp.stack([from_tm(f, FFN_CONV_W - 1) for f in o_s['ffn']]),
    )
    return outs
```

```python
import functools
import math

import numpy as np
import jax
import jax.numpy as jnp
from jax import lax
from jax.experimental import pallas as pl
from jax.experimental.pallas import tpu as pltpu

F32 = jnp.float32
BF16 = jnp.bfloat16

N_MIXERS = 3
H_A = 8
DH_A = 64
DV_A = 128
ROPE_THETA = 10000.0
PAGE_SIZE = 128
NEG_INF = -1e30
HEAD_B = 64
GN_EPS_B = 64e-5
LN_EPS = 1e-5
RMS_EPS = 1e-6
CONV_W = 31
FFN_CONV_W = 3

SUBLANES = 8
LANES = 128
VMEM_LIMIT_BYTES = 56 * 1024 * 1024

ATTN_RC = 256
ATTN_KC = 256
ATTN_DEPTH = 5
VT_ROWS = DV_A + 16
WKV_SEQS_PER_STEP = 4


def _cparams(*sem):
    return pltpu.CompilerParams(dimension_semantics=sem, vmem_limit_bytes=VMEM_LIMIT_BYTES)


def _resident(shape):
    nd = len(shape)
    return pl.BlockSpec(shape, lambda *_: (0,) * nd, pipeline_mode=pl.Buffered(1))


def _dot(a, b):
    return jnp.dot(a, b, preferred_element_type=F32)


def _dot_nt(a, b):
    return lax.dot_general(a, b, (((1,), (1,)), ((), ())), preferred_element_type=F32)


def _dot_tn(a, b):
    return lax.dot_general(a, b, (((0,), (0,)), ((), ())), preferred_element_type=F32)


def _sigmoid(x):
    return 1.0 / (1.0 + jnp.exp(-x))


def _silu(x):
    return x * _sigmoid(x)


def _bc_rows(m, tm):
    nb, c = m.shape
    if nb == 1 or nb == tm:
        return m
    return jnp.broadcast_to(m[None], (tm // nb, nb, c)).reshape(tm, c)


def _modnorm(x, gain, shift, scale):
    tm = x.shape[0]
    y = x * lax.rsqrt(jnp.mean(x * x, axis=-1, keepdims=True) + RMS_EPS) * gain
    return y * (1.0 + _bc_rows(scale, tm)) + _bc_rows(shift, tm)


class _Group:
    def __init__(self, nb, rows, sd, mod, has_state):
        self.nb = nb
        self.rows = rows
        self.sd = sd
        self.mod = mod
        self.has_state = has_state

    def mod_spec(self, layer, k, d):
        if self.mod.ndim == 4:
            return pl.BlockSpec((None, None, 1, d), lambda b, j: (layer, b, 0, k))
        return pl.BlockSpec((None, self.mod.shape[1], d), lambda b, j: (layer, 0, k))

    def row_spec(self, tm, c):
        return pl.BlockSpec((None, tm, c), lambda b, j: (b, j, 0))


def _adaln_body(c_ref, w_ref, b_ref, o_ref):
    sc = _silu(c_ref[...]).astype(BF16)
    o_ref[...] = _dot(sc, w_ref[...].astype(BF16)) + b_ref[...]


def _adaln(c_all, w_ada, b_ada):
    depth, d, n = w_ada.shape
    bp = c_all.shape[0]
    tn = 1536
    return pl.pallas_call(
        _adaln_body,
        grid=(depth, n // tn),
        in_specs=[pl.BlockSpec((bp, d), lambda i, j: (0, 0)),
                  pl.BlockSpec((None, d, tn), lambda i, j: (i, 0, j)),
                  pl.BlockSpec((None, 1, tn), lambda i, j: (i, 0, j))],
        out_specs=pl.BlockSpec((None, bp, tn), lambda i, j: (i, 0, j)),
        out_shape=jax.ShapeDtypeStruct((depth, bp, n), F32),
        compiler_params=_cparams("arbitrary", "arbitrary"),
        name="adaln",
    )(c_all, w_ada, b_ada.reshape(depth, 1, n))


def _pre_a_body(x_ref, sh_ref, sc_ref, gain_ref, w_ref, cos_ref, sina_ref, sinb_ref, *rest,
                head_major, q_scale, n_prev):
    if n_prev:
        kprev_ref, vprev_ref, k_all, v_all, q_out, kh_out, vh_out = rest
        k_all[0:n_prev] = kprev_ref[...]
        v_all[0:n_prev] = vprev_ref[...]
    else:
        k_all, v_all, q_out, kh_out, vh_out = rest
    k_out = k_all.at[n_prev] if head_major else k_all
    v_out = v_all.at[n_prev] if head_major else v_all
    h = _modnorm(x_ref[...], gain_ref[...], sh_ref[...], sc_ref[...]).astype(BF16)
    qkv = _dot(h, w_ref[...])
    cos, sina, sinb = cos_ref[...], sina_ref[...], sinb_ref[...]
    nqk = H_A * LANES
    for s in range(2 * H_A):
        xs = qkv[:, LANES * s:LANES * (s + 1)]
        ro = xs * cos + pltpu.roll(xs, LANES - 32, 1) * sina + pltpu.roll(xs, 32, 1) * sinb
        if s < H_A:
            if head_major:
                q_out[s] = (ro * q_scale).astype(BF16)
            else:
                q_out[:, LANES * s:LANES * (s + 1)] = ro * q_scale
        else:
            hh = s - H_A
            k_out[:, LANES * hh:LANES * (hh + 1)] = ro
            if head_major:
                kh_out[hh] = ro.astype(BF16)
    v = qkv[:, 2 * nqk:]
    v_out[...] = v
    if head_major:
        for hh in range(H_A):
            vh_out[hh, 0:DV_A, :] = v[:, LANES * hh:LANES * (hh + 1)].T.astype(BF16)
            vh_out[hh, DV_A:VT_ROWS, :] = jnp.ones((VT_ROWS - DV_A, v.shape[0]), BF16)


def _rope_tables(pos):
    half = DH_A // 2
    inv = jnp.exp(-math.log(ROPE_THETA) * jnp.arange(half, dtype=F32) * (2.0 / DH_A))
    ang = pos.astype(F32)[:, None] * inv[None, :]
    cos, sin = jnp.cos(ang), jnp.sin(ang)
    zero = jnp.zeros_like(sin)
    cos_t = jnp.concatenate([cos, cos, cos, cos], axis=1)
    sina = jnp.concatenate([-sin, zero, -sin, zero], axis=1)
    sinb = jnp.concatenate([zero, sin, zero, sin], axis=1)
    return cos_t, sina, sinb


def _pre_a(grp, layer, x, gain, w_qkv, tabs, tm, head_major, prev_kv=None):
    nb, rows, d = x.shape
    n = w_qkv.shape[1]
    dk = H_A * LANES
    n_prev = prev_kv[0].shape[0] if prev_kv is not None else 0
    q_scale = DH_A ** -0.5 * (math.log2(math.e) if head_major else 1.0)
    kern = functools.partial(_pre_a_body, head_major=head_major, q_scale=q_scale, n_prev=n_prev)
    tab_spec = pl.BlockSpec((tm, LANES), lambda b, j: (j, 0))
    ins = [x, grp.mod, grp.mod, gain, w_qkv, *tabs]
    in_specs = [grp.row_spec(tm, d), grp.mod_spec(layer, 0, d), grp.mod_spec(layer, 1, d),
                _resident((1, d)), _resident((d, n)), tab_spec, tab_spec, tab_spec]
    if head_major:
        stack_spec = lambda m: pl.BlockSpec((m, None, tm, dk), lambda b, j: (0, b, j, 0))
        if n_prev:
            ins += list(prev_kv)
            in_specs += [stack_spec(n_prev)] * 2
        out_specs = [stack_spec(n_prev + 1)] * 2
        out_shape = [jax.ShapeDtypeStruct((n_prev + 1, nb, rows, dk), F32)] * 2
    else:
        out_specs = [grp.row_spec(tm, dk), grp.row_spec(tm, dk)]
        out_shape = [jax.ShapeDtypeStruct((nb, rows, dk), F32)] * 2
    if head_major:
        hspec = pl.BlockSpec((None, H_A, tm, LANES), lambda b, j: (b, 0, j, 0))
        tspec = pl.BlockSpec((None, H_A, VT_ROWS, tm), lambda b, j: (b, 0, 0, j))
        out_specs += [hspec, hspec, tspec]
        out_shape += [jax.ShapeDtypeStruct((nb, H_A, rows, LANES), BF16)] * 2
        out_shape += [jax.ShapeDtypeStruct((nb, H_A, VT_ROWS, rows), BF16)]
    else:
        dummy = pl.BlockSpec((None, SUBLANES, LANES), lambda b, j: (b, 0, 0))
        out_specs += [grp.row_spec(tm, dk), dummy, dummy]
        out_shape += [jax.ShapeDtypeStruct((nb, rows, dk), F32),
                      jax.ShapeDtypeStruct((nb, SUBLANES, LANES), BF16),
                      jax.ShapeDtypeStruct((nb, SUBLANES, LANES), BF16)]
    return pl.pallas_call(
        kern, grid=(nb, rows // tm), in_specs=in_specs, out_specs=out_specs, out_shape=out_shape,
        compiler_params=_cparams("arbitrary", "arbitrary"), name="pre_attn",
    )(*ins)


def _lam(lq1, lk1, lq2, lk2, lam_init):
    return (jnp.exp(jnp.sum(lq1 * lk1, axis=-1, keepdims=True))
            - jnp.exp(jnp.sum(lq2 * lk2, axis=-1, keepdims=True)) + lam_init)


def _diff_finish(acc, l, lam, g, lam_init, n):
    o = acc[0:n] / l[0:n] - lam * (acc[n:2 * n] / l[n:2 * n])
    return o * lax.rsqrt(jnp.mean(o * o, axis=-1, keepdims=True) + RMS_EPS) * g * (1.0 - lam_init)


def _attn_p_body(qi_tab, ki_tab, q_ref, k_ref, vt_ref, lq1, lk1, lq2, lk2, g_ref, o_ref,
                 qs_sc, m_sc, acc_sc, *, tq, rc, kc, depth, lam_init):
    t = pl.program_id(2)
    qi = qi_tab[t]
    ki = ki_tab[t]

    @pl.when(ki == 0)
    def _():
        q = q_ref[...]
        lane = lax.broadcasted_iota(jnp.int32, q.shape, 1)
        zero = jnp.zeros_like(q)
        qs_sc[0:tq, :] = jnp.where(lane < DH_A, q, zero)
        qs_sc[tq:2 * tq, :] = jnp.where(lane >= DH_A, q, zero)
        m_sc[...] = jnp.full(m_sc.shape, NEG_INF, F32)
        acc_sc[...] = jnp.zeros(acc_sc.shape, F32)

    def scores(blk):
        r0, k0, masked = blk
        st = _dot_nt(k_ref[k0:k0 + kc, :], qs_sc[r0:r0 + rc, :])
        if masked:
            kpos = lax.broadcasted_iota(jnp.int32, st.shape, 0) + k0
            qpos = lax.broadcasted_iota(jnp.int32, st.shape, 1) + (r0 % tq)
            st = jnp.where(kpos <= qpos, st, NEG_INF)
        return st

    def softmax_pv(blk, st):
        r0, k0, _ = blk
        cols = slice(r0, r0 + rc)
        m_prev = m_sc[:, cols]
        m_new = jnp.maximum(m_prev, jnp.max(st, axis=0, keepdims=True))
        alpha = jnp.exp2(m_prev - m_new)
        pt = jnp.exp2(st - m_new).astype(BF16)
        m_sc[:, cols] = m_new
        acc_sc[:, cols] = alpha * acc_sc[:, cols] + _dot(vt_ref[:, k0:k0 + kc], pt)

    def run(blocks):
        pending = []
        for blk in blocks:
            pending.append((blk, scores(blk)))
            if len(pending) > depth:
                softmax_pv(*pending.pop(0))
        for item in pending:
            softmax_pv(*item)

    @pl.when(ki < qi)
    def _():
        run([(r0, k0, False) for k0 in range(0, tq, kc) for r0 in range(0, 2 * tq, rc)])

    @pl.when(ki == qi)
    def _():
        blocks = []
        for k0 in range(0, tq, kc):
            for r0 in range(0, 2 * tq, rc):
                q_lo = r0 % tq
                if k0 <= q_lo + rc - 1:
                    blocks.append((r0, k0, k0 + kc - 1 > q_lo))
        run(blocks)
        lam = _lam(lq1[...], lk1[...], lq2[...], lk2[...], lam_init)
        l_tot = acc_sc[DV_A:DV_A + 1, :]
        acc = acc_sc[0:DV_A, :]
        ot = acc[:, 0:tq] / l_tot[:, 0:tq] - lam * (acc[:, tq:2 * tq] / l_tot[:, tq:2 * tq])
        ot = (ot * lax.rsqrt(jnp.mean(ot * ot, axis=0, keepdims=True) + RMS_EPS)
              * g_ref[...] * (1.0 - lam_init))
        o_ref[...] = ot.T.astype(BF16)


def _attn_prompt(qh, kh, vth, lq1, lk1, lq2, lk2, g, lam_init, tq):
    nb, nh, s, _ = qh.shape
    nq = s // tq
    rc = min(ATTN_RC, tq)
    kc = min(ATTN_KC, tq)
    pairs = [(qi, ki) for qi in range(nq) for ki in range(qi + 1)]
    qi_tab = jnp.asarray(np.array([p[0] for p in pairs], np.int32))
    ki_tab = jnp.asarray(np.array([p[1] for p in pairs], np.int32))
    small = pl.BlockSpec((1, DH_A), lambda b, h, t, qt, kt: (0, 0))
    grid_spec = pltpu.PrefetchScalarGridSpec(
        num_scalar_prefetch=2,
        grid=(nb, nh, len(pairs)),
        in_specs=[pl.BlockSpec((None, None, tq, LANES), lambda b, h, t, qt, kt: (b, h, qt[t], 0)),
                  pl.BlockSpec((None, None, tq, LANES), lambda b, h, t, qt, kt: (b, h, kt[t], 0)),
                  pl.BlockSpec((None, None, VT_ROWS, tq), lambda b, h, t, qt, kt: (b, h, 0, kt[t])),
                  small, small, small, small,
                  pl.BlockSpec((DV_A, 1), lambda b, h, t, qt, kt: (0, 0))],
        out_specs=pl.BlockSpec((None, tq, LANES), lambda b, h, t, qt, kt: (b, qt[t], h)),
        scratch_shapes=[pltpu.VMEM((2 * tq, LANES), BF16), pltpu.VMEM((1, 2 * tq), F32),
                        pltpu.VMEM((VT_ROWS, 2 * tq), F32)])
    return pl.pallas_call(
        functools.partial(_attn_p_body, tq=tq, rc=rc, kc=kc, depth=ATTN_DEPTH, lam_init=lam_init),
        grid_spec=grid_spec,
        out_shape=jax.ShapeDtypeStruct((nb, s, nh * LANES), BF16),
        compiler_params=_cparams("arbitrary", "arbitrary", "arbitrary"), name="attn_prompt",
    )(qi_tab, ki_tab, qh, kh, vth, lq1, lk1, lq2, lk2, g)


def _attn_s_body(pt, q_ref, kn_ref, vn_ref, *rest, n_pages, lam_init):
    k_refs = rest[:n_pages]
    v_refs = rest[n_pages:2 * n_pages]
    lq1, lk1, lq2, lk2, g_ref, o_ref = rest[2 * n_pages:]
    t_new, dk = q_ref.shape
    n_sub = 2 * H_A
    n_rows = n_sub * t_new
    page = k_refs[0].shape[1]

    q_tiled = jnp.concatenate([q_ref[...]] * n_sub, axis=0)
    ri = lax.broadcasted_iota(jnp.int32, (n_rows, dk), 0)
    ci = lax.broadcasted_iota(jnp.int32, (n_rows, dk), 1)
    same_sub = (lax.shift_right_logical(ri, int(math.log2(t_new)))
                == lax.shift_right_logical(ci, int(math.log2(DH_A))))
    qbd = jnp.where(same_sub, q_tiled, jnp.zeros_like(q_tiled)).astype(BF16)

    s_past = jnp.concatenate([_dot(qbd, k_refs[p][...].astype(BF16)) for p in range(n_pages)], axis=1)
    s_new = _dot_nt(qbd, kn_ref[...].astype(BF16))
    row = lax.broadcasted_iota(jnp.int32, s_new.shape, 0)
    col = lax.broadcasted_iota(jnp.int32, s_new.shape, 1)
    s_new = jnp.where(col <= jnp.bitwise_and(row, t_new - 1), s_new, NEG_INF)
    m = jnp.maximum(jnp.max(s_past, axis=1, keepdims=True), jnp.max(s_new, axis=1, keepdims=True))
    p_past = jnp.exp(s_past - m)
    p_new = jnp.exp(s_new - m)
    l = jnp.sum(p_past, axis=1, keepdims=True) + jnp.sum(p_new, axis=1, keepdims=True)
    p_past = p_past.astype(BF16)

    lam = _lam(lq1[...], lk1[...], lq2[...], lk2[...], lam_init)
    vn = vn_ref[...]
    for hh in range(H_A):
        rows = slice(2 * t_new * hh, 2 * t_new * (hh + 1))
        v_h = jnp.concatenate([v_refs[p][pl.ds(hh, page, stride=H_A), :] for p in range(n_pages)],
                              axis=0).astype(BF16)
        acc = _dot(p_past[rows, :], v_h) + _dot(p_new[rows, :], vn[:, LANES * hh:LANES * (hh + 1)])
        o = _diff_finish(acc, l[rows, :], lam, g_ref[...], lam_init, t_new)
        o_ref[:, LANES * hh:LANES * (hh + 1)] = o.astype(BF16)


def _attn_sample(q, k_new, v_new, cache_kt, cache_v2, page_table, j, lq1, lk1, lq2, lk2, g, lam_init):
    t_new, width = q.shape
    db, n_pages = page_table.shape
    dk = width // db
    small = pl.BlockSpec((1, DH_A), lambda b, pt: (0, 0))
    seq = pl.BlockSpec((t_new, dk), lambda b, pt: (0, b))

    def page_spec(arr, p):
        return pl.BlockSpec((None, None) + arr.shape[2:], lambda b, pt: (j, pt[b, p], 0, 0))

    grid_spec = pltpu.PrefetchScalarGridSpec(
        num_scalar_prefetch=1,
        grid=(db,),
        in_specs=([seq, seq, seq] + [page_spec(cache_kt, p) for p in range(n_pages)]
                  + [page_spec(cache_v2, p) for p in range(n_pages)]
                  + [small, small, small, small, pl.BlockSpec((1, DV_A), lambda b, pt: (0, 0))]),
        out_specs=seq)
    return pl.pallas_call(
        functools.partial(_attn_s_body, n_pages=n_pages, lam_init=lam_init),
        grid_spec=grid_spec,
        out_shape=jax.ShapeDtypeStruct((t_new, width), BF16),
        compiler_params=_cparams("arbitrary"), name="attn_sample",
    )(page_table, q, k_new, v_new, *([cache_kt] * n_pages), *([cache_v2] * n_pages),
      lq1, lk1, lq2, lk2, g)


def _pre_b_body(*refs, sd, halo, has_state):
    (x_ref, sh_ref, sc_ref, gain_ref, mu_ref, wr, wk, wv, w0, w1, w2, a0, a1, a2, g1, g2) = refs[:16]
    rest = refs[16:]
    if has_state:
        st_ref, rest = rest[0], rest[1:]
    r_out, wl_out, k_out, v_out, a_out, g_out, hl_out, ext = rest
    j = pl.program_id(1)
    tm = x_ref.shape[0]

    @pl.when(j == 0)
    def _():
        if has_state:
            ext[0:halo, :] = st_ref[...]
        else:
            ext[0:halo, :] = jnp.zeros((halo, ext.shape[1]), F32)

    h = _modnorm(x_ref[...], gain_ref[...], sh_ref[...], sc_ref[...])
    ext[halo:halo + tm, :] = h
    xx = ext[pl.ds(halo - sd, tm), :] - h
    ext[0:halo, :] = ext[tm:tm + halo, :]
    hl_out[...] = h[tm - halo:tm]

    mu = mu_ref[...]
    mix = lambda n: (h + xx * mu[n:n + 1]).astype(BF16)
    r_out[...] = _dot(mix(0), wr[...])
    wraw = w0[...] + _dot(jnp.tanh(_dot(mix(1), w1[...])).astype(BF16), w2[...])
    k_out[...] = _dot(mix(2), wk[...])
    v_out[...] = _dot(mix(3), wv[...])
    a_out[...] = _sigmoid(a0[...] + _dot(_dot(mix(4), a1[...]).astype(BF16), a2[...]))
    g_out[...] = _dot(_sigmoid(_dot(mix(5), g1[...])).astype(BF16), g2[...])
    z = -wraw
    softplus = jnp.maximum(z, 0.0) + jnp.log(1.0 + jnp.exp(-jnp.abs(z)))
    wl_out[...] = -jnp.exp(-softplus - 0.5)


def _pre_b(grp, layer, x, gain, prm, state, tm):
    nb, rows, d = x.shape
    halo = max(SUBLANES, grp.sd)
    kern = functools.partial(_pre_b_body, sd=grp.sd, halo=halo, has_state=grp.has_state)
    row = lambda a: a.reshape(1, -1)
    ins = [x, grp.mod, grp.mod, gain, prm['mu'], prm['w_r'], prm['w_k'], prm['w_v'], row(prm['w0']),
           prm['w1'], prm['w2'], row(prm['a0']), prm['a1'], prm['a2'], prm['g1'], prm['g2']]
    in_specs = [grp.row_spec(tm, d), grp.mod_spec(layer, 0, d), grp.mod_spec(layer, 1, d)]
    in_specs += [_resident(a.shape) for a in ins[3:]]
    if grp.has_state:
        ins.append(state)
        in_specs.append(_resident(state.shape))
    big = jax.ShapeDtypeStruct((nb, rows, d), F32)
    out_specs = [grp.row_spec(tm, d)] * 6 + [pl.BlockSpec((None, halo, d), lambda b, j: (b, 0, 0))]
    out_shape = [big] * 6 + [jax.ShapeDtypeStruct((nb, halo, d), F32)]
    return pl.pallas_call(
        kern, grid=(nb, rows // tm), in_specs=in_specs, out_specs=out_specs, out_shape=out_shape,
        scratch_shapes=[pltpu.VMEM((halo + tm, d), F32)],
        compiler_params=_cparams("arbitrary", "arbitrary"), name="pre_rwkv",
    )(*ins)


def _wkv_body(*refs, L, has_state, mm_dtype):
    (r_ref, wl_ref, k_ref, v_ref, a_ref, g_ref, kk_ref, ka_ref, rk_ref, lng_ref, lnb_ref) = refs[:11]
    rest = refs[11:]
    if has_state:
        s0_ref, rest = rest[0], rest[1:]
    mo_ref, sout_ref, g_sc = rest
    c = pl.program_id(1)
    d = kk_ref.shape[1]
    batched = len(r_ref.shape) == 3
    nbb = r_ref.shape[0] if batched else r_ref.shape[1] // d
    n_pairs = d // LANES
    hb = HEAD_B
    mm = lambda x: x.astype(mm_dtype)
    seq = lambda ref: ([ref[bb] for bb in range(nbb)] if batched
                       else [ref[:, d * bb:d * (bb + 1)] for bb in range(nbb)])
    units = [(bb, pr) for bb in range(nbb) for pr in range(n_pairs)]

    @pl.when(c == 0)
    def _():
        if has_state:
            zero = jnp.zeros((hb, hb), F32)
            for i, (bb, pr) in enumerate(units):
                top = jnp.concatenate([s0_ref[bb, 2 * pr], zero], axis=1)
                bot = jnp.concatenate([zero, s0_ref[bb, 2 * pr + 1]], axis=1)
                g_sc[i] = jnp.concatenate([top, bot], axis=0)
        else:
            g_sc[...] = jnp.zeros(g_sc.shape, F32)

    ti = lax.broadcasted_iota(jnp.int32, (L, L), 0)
    tj = lax.broadcasted_iota(jnp.int32, (L, L), 1)
    tri = jnp.where(tj <= ti, 1.0, 0.0).astype(F32)

    def cumsum_time(w):
        if mm_dtype == F32:
            return jnp.dot(tri, w, preferred_element_type=F32, precision=lax.Precision.HIGHEST)
        tri_b = tri.astype(BF16)
        hi = w.astype(BF16)
        rem = w - hi.astype(F32)
        mid = rem.astype(BF16)
        lo = (rem - mid.astype(F32)).astype(BF16)
        return _dot(tri_b, hi) + _dot(tri_b, mid) + _dot(tri_b, lo)

    wl_b = seq(wl_ref)
    cum_b = [cumsum_time(w) for w in wl_b]
    gam_b = [jnp.exp(cm) for cm in cum_b]
    gam_prev_b = [jnp.exp(cum_b[bb] - wl_b[bb]) for bb in range(nbb)]
    inv_gam_b = [jnp.exp(-cm) for cm in cum_b]

    lane = lax.broadcasted_iota(jnp.int32, (L, LANES), 1)
    in_h0 = lane < hb
    bi = lax.broadcasted_iota(jnp.int32, (LANES, LANES), 0)
    bj = lax.broadcasted_iota(jnp.int32, (LANES, LANES), 1)
    same_head = (bi < hb) == (bj < hb)
    ones_bd = jnp.where(same_head, 1.0, 0.0).astype(BF16)

    def headsum(x):
        return _dot(x.astype(BF16), ones_bd)

    def stack(x):
        zero = jnp.zeros_like(x)
        return jnp.concatenate([jnp.where(in_h0, x, zero), jnp.where(in_h0, zero, x)], axis=0)

    wr = lax.broadcasted_iota(jnp.int32, (L, 2 * L), 0)
    wc = lax.broadcasted_iota(jnp.int32, (L, 2 * L), 1)
    wt = jnp.where(wc >= L, wc - L, wc)
    strict = wt < wr
    incl = wt <= wr
    w_h0 = wc < L

    def bd(w):
        zero = jnp.zeros_like(w)
        return jnp.concatenate([jnp.where(w_h0, w, zero), jnp.where(w_h0, zero, w)], axis=0)

    r_b, k_b, v_b, a_b, g_b = seq(r_ref), seq(k_ref), seq(v_ref), seq(a_ref), seq(g_ref)
    prs = range(len(units))
    sls = [slice(LANES * pr, LANES * (pr + 1)) for _, pr in units]
    per_unit = lambda arrs: [arrs[bb][:, sls[i]] for i, (bb, _) in enumerate(units)]
    a, kraw, v, r, g_gate = per_unit(a_b), per_unit(k_b), per_unit(v_b), per_unit(r_b), per_unit(g_b)
    gam, gam_prev, inv_gam = per_unit(gam_b), per_unit(gam_prev_b), per_unit(inv_gam_b)
    kk = [kraw[i] * kk_ref[:, sls[i]] for i in prs]
    kmod = [kraw[i] * (1.0 + (a[i] - 1.0) * ka_ref[:, sls[i]]) for i in prs]
    ssq = [headsum(kk[i] * kk[i]) for i in prs]
    bonus_w = [headsum(r[i] * kmod[i] * rk_ref[:, sls[i]]) for i in prs]
    kk = [kk[i] * lax.rsqrt(jnp.maximum(ssq[i], 1e-24)) for i in prs]
    P = [-kk[i] * gam_prev[i] for i in prs]
    Q = [kk[i] * a[i] * inv_gam[i] for i in prs]
    K = [kmod[i] * inv_gam[i] for i in prs]
    R = [r[i] * gam[i] for i in prs]

    z = [_dot_nt(mm(jnp.concatenate([P[i], R[i]], axis=0)),
                 mm(jnp.concatenate([stack(Q[i]), stack(K[i])], axis=0))) for i in prs]
    zero_w = jnp.zeros((L, 2 * L), F32)
    a_pq = [jnp.where(strict, z[i][0:L, 0:2 * L], zero_w) for i in prs]
    a_pk = [jnp.where(strict, z[i][0:L, 2 * L:4 * L], zero_w) for i in prs]
    a_rq = [jnp.where(incl, z[i][L:2 * L, 0:2 * L], zero_w) for i in prs]
    a_rk = [jnp.where(incl, z[i][L:2 * L, 2 * L:4 * L], zero_w) for i in prs]

    nmat = list(a_pq)
    pw = list(a_pq)
    for _ in range(int(math.log2(L)) - 1):
        pw = [_dot(mm(pw[i]), mm(bd(pw[i]))) for i in prs]
        nmat = [nmat[i] + pw[i] + _dot(mm(nmat[i]), mm(bd(pw[i]))) for i in prs]

    gmat = [g_sc[i] for i in prs]
    gb = [mm(gmat[i]) for i in prs]
    sv = [mm(stack(v[i])) for i in prs]
    rhs = [_dot_nt(mm(P[i]), gb[i]) + _dot(mm(a_pk[i]), sv[i]) for i in prs]
    u = [rhs[i] + _dot(mm(nmat[i]), mm(stack(rhs[i]))) for i in prs]
    y = [_dot_nt(mm(R[i]), gb[i]) + _dot(mm(a_rq[i]), mm(stack(u[i]))) + _dot(mm(a_rk[i]), sv[i])
         for i in prs]
    dg = [_dot_tn(mm(jnp.concatenate([u[i], v[i]], axis=0)), mm(jnp.concatenate([Q[i], K[i]], axis=0)))
          for i in prs]
    for i in prs:
        g_sc[i] = (gmat[i] + jnp.where(same_head, dg[i], jnp.zeros_like(dg[i]))) * gam[i][L - 1:L]

    mean = [headsum(y[i]) * (1.0 / hb) for i in prs]
    dlt = [y[i] - mean[i] for i in prs]
    var = [headsum(dlt[i] * dlt[i]) * (1.0 / hb) for i in prs]
    for i, (bb, _) in enumerate(units):
        yn = dlt[i] * lax.rsqrt(var[i] + GN_EPS_B) * lng_ref[:, sls[i]] + lnb_ref[:, sls[i]]
        out = ((yn + bonus_w[i] * v[i]) * g_gate[i]).astype(BF16)
        if batched:
            mo_ref[bb, :, sls[i]] = out
        else:
            mo_ref[:, d * bb + sls[i].start:d * bb + sls[i].stop] = out

    @pl.when(c == pl.num_programs(1) - 1)
    def _():
        for i, (bb, pr) in enumerate(units):
            gfin = g_sc[i]
            sout_ref[bb, 2 * pr] = gfin[0:hb, 0:hb]
            sout_ref[bb, 2 * pr + 1] = gfin[hb:2 * hb, hb:2 * hb]


def _wkv(seqs, prm, state, L, time_major):
    row = lambda a: a.reshape(1, -1)
    d = prm['k_k'].shape[-1]
    n_heads = d // HEAD_B
    if time_major:
        t_len, width = seqs[0].shape
        nb = width // d
        nbb = WKV_SEQS_PER_STEP
        seq_spec = pl.BlockSpec((L, nbb * d), lambda b, c: (c, b))
        mo_shape = jax.ShapeDtypeStruct((t_len, width), BF16)
    else:
        nb, t_len, _ = seqs[0].shape
        nbb = nb
        seq_spec = pl.BlockSpec((nbb, L, d), lambda b, c: (b, c, 0))
        mo_shape = jax.ShapeDtypeStruct((nb, t_len, d), BF16)
    has_state = state is not None
    st_spec = pl.BlockSpec((nbb, n_heads, HEAD_B, HEAD_B), lambda b, c: (b, 0, 0, 0))
    ins = list(seqs) + [row(prm['k_k']), row(prm['k_a']), row(prm['r_k']), row(prm['ln_g']), row(prm['ln_b'])]
    in_specs = [seq_spec] * 6 + [_resident((1, d))] * 5
    if has_state:
        ins.append(state)
        in_specs.append(st_spec)
    kern = functools.partial(_wkv_body, L=L, has_state=has_state, mm_dtype=BF16 if L >= 16 else F32)
    return pl.pallas_call(
        kern, grid=(nb // nbb, t_len // L), in_specs=in_specs,
        out_specs=[seq_spec, st_spec],
        out_shape=[mo_shape, jax.ShapeDtypeStruct((nb, n_heads, HEAD_B, HEAD_B), F32)],
        scratch_shapes=[pltpu.VMEM((nbb * n_heads // 2, LANES, LANES), F32)],
        compiler_params=_cparams("arbitrary", "arbitrary"), name="wkv",
    )(*ins)


def _layernorm_silu(y, g, b):
    mu = jnp.mean(y, axis=-1, keepdims=True)
    dlt = y - mu
    var = jnp.mean(dlt * dlt, axis=-1, keepdims=True)
    return _silu(dlt * lax.rsqrt(var + LN_EPS) * g + b)


def _pre_c_prompt_body(x_ref, sh_ref, sc_ref, gain_ref, w1, b1, wdw, bdw, lng, lnb,
                       mo_ref, cache_ref, ext, shift_sc, *, halo):
    j = pl.program_id(1)
    tm, d = x_ref.shape

    @pl.when(j == 0)
    def _():
        ext[0:halo, :] = jnp.zeros((halo, d), F32)

    h = _modnorm(x_ref[...], gain_ref[...], sh_ref[...], sc_ref[...]).astype(BF16)
    u = _dot(h, w1[...]) + b1[...]
    glu = u[:, :d] * _sigmoid(u[:, d:])
    ext[halo:halo + tm, :] = glu
    span = halo - SUBLANES
    cols_out = []
    for lt in range(d // LANES):
        cols = slice(LANES * lt, LANES * (lt + 1))
        acc = None
        for b in range(SUBLANES):
            shift_sc[b] = ext[pl.ds(SUBLANES - b, tm + span), cols]
            for a in range(halo // SUBLANES):
                dd = SUBLANES * a + b
                if dd >= CONV_W:
                    continue
                off = span - SUBLANES * a
                term = shift_sc[b, off:off + tm, :] * wdw[CONV_W - 1 - dd:CONV_W - dd, cols]
                acc = term if acc is None else acc + term
        cols_out.append(acc)
    y = jnp.concatenate(cols_out, axis=1) + bdw[...]
    tail = ext[tm:tm + halo, :]
    ext[0:halo, :] = tail
    cache_ref[...] = tail
    mo_ref[...] = _layernorm_silu(y, lng[...], lnb[...]).astype(BF16)


def _pre_c_sample_body(x_ref, sh_ref, sc_ref, gain_ref, w1, b1, wdw, bdw, lng, lnb, st_ref,
                       mo_ref, glu_ref, *, nb):
    tm, d = x_ref.shape
    n_t = tm // nb
    n_st = CONV_W - 1
    h = _modnorm(x_ref[...], gain_ref[...], sh_ref[...], sc_ref[...]).astype(BF16)
    u = _dot(h, w1[...]) + b1[...]
    glu = u[:, :d] * _sigmoid(u[:, d:])
    glu_ref[...] = glu
    for t in range(n_t):
        acc = None
        for w in range(CONV_W):
            pos = t + w
            if pos < n_st:
                slab = st_ref[nb * pos:nb * (pos + 1), :]
            else:
                slab = glu[nb * (pos - n_st):nb * (pos - n_st + 1), :]
            term = slab * wdw[w:w + 1, :]
            acc = term if acc is None else acc + term
        mo_ref[nb * t:nb * (t + 1), :] = _layernorm_silu(acc + bdw[...], lng[...], lnb[...]).astype(BF16)


def _pre_c(grp, layer, x, gain, prm, state, tm):
    nb, rows, d = x.shape
    row = lambda a: a.reshape(1, -1)
    ins = [x, grp.mod, grp.mod, gain, prm['w_pw1'], row(prm['b_pw1']), prm['w_dw'], row(prm['b_dw']),
           row(prm['ln_g']), row(prm['ln_b'])]
    in_specs = [grp.row_spec(tm, d), grp.mod_spec(layer, 0, d), grp.mod_spec(layer, 1, d)]
    in_specs += [_resident(a.shape) for a in ins[3:]]
    if grp.has_state:
        db = grp.sd
        ins.append(state)
        in_specs.append(_resident(state.shape))
        return pl.pallas_call(
            functools.partial(_pre_c_sample_body, nb=db), grid=(nb, 1), in_specs=in_specs,
            out_specs=[grp.row_spec(rows, d), grp.row_spec(rows, d)],
            out_shape=[jax.ShapeDtypeStruct((nb, rows, d), BF16), jax.ShapeDtypeStruct((nb, rows, d), F32)],
            compiler_params=_cparams("arbitrary", "arbitrary"), name="pre_conv_sample",
        )(*ins)
    halo = 32
    return pl.pallas_call(
        functools.partial(_pre_c_prompt_body, halo=halo), grid=(nb, rows // tm), in_specs=in_specs,
        out_specs=[grp.row_spec(tm, d), pl.BlockSpec((None, halo, d), lambda b, j: (b, 0, 0))],
        out_shape=[jax.ShapeDtypeStruct((nb, rows, d), BF16), jax.ShapeDtypeStruct((nb, halo, d), F32)],
        scratch_shapes=[pltpu.VMEM((halo + tm, d), F32),
                        pltpu.VMEM((SUBLANES, tm + halo - SUBLANES, LANES), F32)],
        compiler_params=_cparams("arbitrary", "arbitrary"), name="pre_conv_prompt",
    )(*ins)


def _post_ffn_body(*refs, sd, halo, has_state, final):
    (x_ref, mo_ref, g1_ref, sh_ref, sc_ref, g2_ref, wo, bo, gain_ref, w_in, wdw, bdw, w_out,
     nout_ref) = refs[:14]
    rest = refs[14:]
    if has_state:
        st_ref, rest = rest[0], rest[1:]
    y_ref, cache_ref, ext = rest
    j = pl.program_id(1)
    tm = x_ref.shape[0]
    dff = ext.shape[1]

    @pl.when(j == 0)
    def _():
        if has_state:
            ext[0:halo, :] = st_ref[...]
        else:
            ext[0:halo, :] = jnp.zeros((halo, dff), F32)

    x1 = x_ref[...] + _bc_rows(g1_ref[...], tm) * (_dot(mo_ref[...], wo[...]) + bo[...])
    h = _modnorm(x1, gain_ref[...], sh_ref[...], sc_ref[...]).astype(BF16)
    u = _dot(h, w_in[...])
    act = u[:, :dff]
    val = u[:, dff:]
    ext[halo:halo + tm, :] = act
    conv = (act * wdw[2:3, :] + ext[pl.ds(halo - sd, tm), :] * wdw[1:2, :]
            + ext[pl.ds(halo - 2 * sd, tm), :] * wdw[0:1, :] + bdw[...])
    tail = ext[tm:tm + halo, :]
    ext[0:halo, :] = tail
    cache_ref[...] = tail
    gated = (_silu(conv) * val).astype(BF16)
    x2 = x1 + _bc_rows(g2_ref[...], tm) * _dot(gated, w_out[...])
    if final:
        x2 = x2 * lax.rsqrt(jnp.mean(x2 * x2, axis=-1, keepdims=True) + RMS_EPS) * nout_ref[...]
    y_ref[...] = x2


def _post_ffn(grp, layer, x, mo, w_o, b_o, gain, prm, state, norm_out, tm, final):
    nb, rows, d = x.shape
    dff = prm['w_dw'].shape[1]
    halo = max(SUBLANES, (FFN_CONV_W - 1) * grp.sd)
    row = lambda a: a.reshape(1, -1)
    ins = [x, mo, grp.mod, grp.mod, grp.mod, grp.mod, w_o, row(b_o), gain, prm['w_in'], prm['w_dw'],
           row(prm['b_dw']), prm['w_out'], row(norm_out)]
    in_specs = [grp.row_spec(tm, d), grp.row_spec(tm, d), grp.mod_spec(layer, 2, d),
                grp.mod_spec(layer, 3, d), grp.mod_spec(layer, 4, d), grp.mod_spec(layer, 5, d)]
    in_specs += [_resident(a.shape) for a in ins[6:]]
    if grp.has_state:
        ins.append(state)
        in_specs.append(_resident(state.shape))
    kern = functools.partial(_post_ffn_body, sd=grp.sd, halo=halo, has_state=grp.has_state, final=final)
    return pl.pallas_call(
        kern, grid=(nb, rows // tm), in_specs=in_specs,
        out_specs=[grp.row_spec(tm, d), pl.BlockSpec((None, halo, dff), lambda b, j: (b, 0, 0))],
        out_shape=[jax.ShapeDtypeStruct((nb, rows, d), F32), jax.ShapeDtypeStruct((nb, halo, dff), F32)],
        scratch_shapes=[pltpu.VMEM((halo + tm, dff), F32)],
        compiler_params=_cparams("arbitrary", "arbitrary"), name="post_ffn",
    )(*ins)


def _trunk(grp, x, pos, P, states, paged, tiles):
    nb, rows, d = x.shape
    depth = P['norm_mix'].shape[0]
    out = dict(k=[], v=[], wkv=[], shift=[], conv=[], ffn=[])
    tabs = _rope_tables(pos)
    for i in range(depth):
        kind, j = i % N_MIXERS, i // N_MIXERS
        gain_mix = P['norm_mix'][i].reshape(1, d)
        zeros_d = jnp.zeros((d,), F32)
        if kind == 0:
            lam_init = 0.8 - 0.6 * math.exp(-0.3 * i)
            lvec = [P[n][j].reshape(1, DH_A) for n in ('a_lq1', 'a_lk1', 'a_lq2', 'a_lk2')]
            subln = P['a_subln'][j].reshape(1, DV_A)
            head_major = paged is None
            prev_kv = (out['k'][0], out['v'][0]) if head_major and out['k'] else None
            k, v, q, kh, vh = _pre_a(grp, i, x, gain_mix, P['a_w_qkv'][j], tabs, tiles['pre_a'], head_major,
                                     prev_kv)
            if paged is None:
                out['k'], out['v'] = [], []
                mo = _attn_prompt(q, kh, vh, *lvec, subln.reshape(DV_A, 1), lam_init, tiles['attn_q'])
            else:
                cache_k, cache_v, page_table = paged
                db = page_table.shape[0]
                t_new = rows // db
                tm_view = lambda a: a.reshape(t_new, db * a.shape[-1])
                mo = _attn_sample(tm_view(q), tm_view(k), tm_view(v), cache_k, cache_v, page_table, j,
                                  *lvec, subln, lam_init).reshape(nb, rows, d)
            out['k'].append(k)
            out['v'].append(v)
            w_o, b_o = P['a_w_o'][j], zeros_d
        elif kind == 1:
            prm = {n[2:]: P[n][j] for n in P if n.startswith('b_')}
            st_shift, st_wkv = (states['shift'][j], states['wkv'][j]) if grp.has_state else (None, None)
            r, wl, k, v, a, g, hl = _pre_b(grp, i, x, gain_mix, prm, st_shift, tiles['pre_b'])
            seqs = (r, wl, k, v, a, g)
            if grp.has_state:
                t_new = rows // grp.sd
                seqs = tuple(s.reshape(t_new, grp.sd * d) for s in seqs)
                mo, wkv = _wkv(seqs, prm, st_wkv, t_new, True)
                mo = mo.reshape(nb, rows, d)
            else:
                mo, wkv = _wkv(seqs, prm, None, tiles['wkv'], False)
            out['shift'].append(hl)
            out['wkv'].append(wkv)
            w_o, b_o = prm['w_o'], zeros_d
        else:
            prm = {n[3:]: P[n][j] for n in P if n.startswith('cv_')}
            st = states['conv'][j] if grp.has_state else None
            mo, cache = _pre_c(grp, i, x, gain_mix, prm, st, tiles['pre_c'])
            out['conv'].append(cache)
            w_o, b_o = prm['w_pw2'], prm['b_pw2']
        prm = {n[2:]: P[n][i] for n in ('f_w_in', 'f_w_dw', 'f_b_dw', 'f_w_out')}
        st = states['ffn'][i] if grp.has_state else None
        x, fcache = _post_ffn(grp, i, x, mo, w_o, b_o, P['norm_ffn'][i].reshape(1, d), prm, st,
                              P['norm_out'], tiles['ffn'], i == depth - 1)
        out['ffn'].append(fcache)
    return x, out


def kernel(x_prompt, x_sample, cache_k, cache_v, state_wkv, state_shift, state_conv, state_ffn, page_table, c_prompt, c_sample, norm_mix, norm_ffn, w_ada, b_ada, a_w_qkv, a_w_o, a_lq1, a_lk1, a_lq2, a_lk2, a_subln, b_mu, b_w_r, b_w_k, b_w_v, b_w_o, b_w0, b_w1, b_w2, b_a0, b_a1, b_a2, b_g1, b_g2, b_k_k, b_k_a, b_r_k, b_ln_g, b_ln_b, cv_w_pw1, cv_b_pw1, cv_w_dw, cv_b_dw, cv_ln_g, cv_ln_b, cv_w_pw2, cv_b_pw2, f_w_in, f_w_dw, f_b_dw, f_w_out, norm_out):
    bsz, seq, d = x_prompt.shape
    db, t_new, _ = x_sample.shape
    depth = norm_mix.shape[0]
    past = page_table.shape[1] * PAGE_SIZE
    bf = lambda a: a.astype(BF16)
    P = dict(norm_mix=norm_mix, norm_ffn=norm_ffn, norm_out=norm_out,
             a_w_qkv=bf(a_w_qkv), a_w_o=bf(a_w_o), a_lq1=a_lq1, a_lk1=a_lk1, a_lq2=a_lq2, a_lk2=a_lk2,
             a_subln=a_subln, b_mu=b_mu, b_w_r=bf(b_w_r), b_w_k=bf(b_w_k), b_w_v=bf(b_w_v),
             b_w_o=bf(b_w_o), b_w0=b_w0, b_w1=bf(b_w1), b_w2=bf(b_w2), b_a0=b_a0, b_a1=bf(b_a1),
             b_a2=bf(b_a2), b_g1=bf(b_g1), b_g2=bf(b_g2), b_k_k=b_k_k, b_k_a=b_k_a, b_r_k=b_r_k,
             b_ln_g=b_ln_g, b_ln_b=b_ln_b, cv_w_pw1=bf(cv_w_pw1), cv_b_pw1=cv_b_pw1, cv_w_dw=cv_w_dw,
             cv_b_dw=cv_b_dw, cv_ln_g=cv_ln_g, cv_ln_b=cv_ln_b, cv_w_pw2=bf(cv_w_pw2),
             cv_b_pw2=cv_b_pw2, f_w_in=bf(f_w_in), f_w_dw=f_w_dw, f_b_dw=f_b_dw, f_w_out=bf(f_w_out))

    pad = (-bsz) % SUBLANES
    c_all = jnp.concatenate([c_prompt, jnp.zeros((pad, d), F32), c_sample], axis=0)
    mod = _adaln(c_all, w_ada, b_ada)
    mod_p = mod[:, :bsz].reshape(depth, bsz, 1, 6 * d)
    mod_s = mod[:, bsz + pad:]

    grp_p = _Group(bsz, seq, 1, mod_p, False)
    tiles_p = dict(pre_a=min(512, seq), attn_q=min(2048, seq), pre_b=min(512, seq), wkv=64,
                   pre_c=min(256, seq), ffn=min(256, seq))
    y_p, o_p = _trunk(grp_p, x_prompt, jnp.arange(seq), P, None, None, tiles_p)

    rows_s = t_new * db
    to_tm = lambda a: jnp.moveaxis(a, 1, 0)
    x_s = to_tm(x_sample).reshape(1, rows_s, d)
    states = dict(
        shift=[state_shift[j] for j in range(state_shift.shape[0])],
        wkv=[state_wkv[j] for j in range(state_wkv.shape[0])],
        conv=[to_tm(state_conv[j]).reshape(-1, d) for j in range(state_conv.shape[0])],
        ffn=[to_tm(state_ffn[i]).reshape(-1, state_ffn.shape[-1]) for i in range(depth)])
    grp_s = _Group(1, rows_s, db, mod_s, True)
    tiles_s = dict(pre_a=rows_s, pre_b=rows_s // 2, pre_c=rows_s, ffn=2 * db)
    pos_s = jnp.repeat(past + jnp.arange(t_new), db)
    n_a, n_pool = cache_k.shape[:2]
    cache_kt = jnp.transpose(cache_k, (0, 1, 3, 4, 2)).reshape(n_a, n_pool, 2 * H_A * DH_A, PAGE_SIZE)
    cache_v2 = cache_v.reshape(n_a, n_pool, PAGE_SIZE * H_A, DV_A)
    y_s, o_s = _trunk(grp_s, x_s, pos_s, P, states, (cache_kt, cache_v2, page_table), tiles_s)

    from_tm = lambda a, w: jnp.moveaxis(a.reshape(w, db, a.shape[-1]), 0, 1)
    dff = f_w_dw.shape[-1]
    n_keep = CONV_W - 1
    outs = (
        y_p,
        from_tm(y_s, t_new),
        o_p['k'][0].reshape(-1, bsz, seq, 2 * H_A, DH_A),
        o_p['v'][0].reshape(-1, bsz, seq, H_A, DV_A),
        jnp.stack(o_p['wkv']),
        jnp.stack([h[:, -1] for h in o_p['shift']]),
        jnp.stack([c[:, -n_keep:] for c in o_p['conv']]),
        jnp.stack([f[:, -(FFN_CONV_W - 1):] for f in o_p['ffn']]),
        jnp.stack([from_tm(k, t_new) for k in o_s['k']]).reshape(-1, db, t_new, 2 * H_A, DH_A),
        jnp.stack([from_tm(v, t_new) for v in o_s['v']]).reshape(-1, db, t_new, H_A, DV_A),
        jnp.stack(o_s['wkv']),
        jnp.stack([h[0] for h in o_s['shift']]),
        jnp.stack([from_tm(jnp.concatenate([st.reshape(n_keep, db, d), glu.reshape(t_new, db, d)], axis=0)
                           [-n_keep:].reshape(-1, d), n_keep)
                   for st, glu in zip(states['conv'], o_s['conv'])]),
        jnp.stack([from_tm(f, FFN_CONV_W - 1) for f in o_s['ffn']]),
    )
    return outs
```
